```python
import math
import jax, jax.numpy as jnp
from jax import lax
import numpy as np

D_MODEL = 1024
BATCH = 1
SEQ = 16384
DEPTH = 2
DEC_BATCH = 16
DEC_SEQ = 32
PAST_LEN = 4096

CHUNK = 64
MIX_WIDTH = D_MODEL
POOL_WIDTH = MIX_WIDTH // 2
POOL_WINDOWS = (2, 4, 8, 16)
POOL_GROUPS = len(POOL_WINDOWS)
POOL_GROUP_WIDTH = POOL_WIDTH // POOL_GROUPS
POOL_HIST = max(POOL_WINDOWS) - 1
SB_WIDTH = MIX_WIDTH - POOL_WIDTH
SB_HEAD_DIM = 64
SB_HEADS = SB_WIDTH // SB_HEAD_DIM
SB_SCALE = 1.0 / math.sqrt(SB_HEAD_DIM)
QBLOCK = 128
D_FF = int(math.ceil(8 * D_MODEL / 3 / 256)) * 256
DEEPNORM_ALPHA = (2 * DEPTH) ** 0.25
DEEPNORM_BETA = (8 * DEPTH) ** -0.25
LN_EPS = 1e-5

kernel_name = "pool_stickbreak_hybrid_stream_step"


def layer_norm(x, gain=None, bias=None):
    xf = x.astype(jnp.float32)
    mu = jnp.mean(xf, axis=-1, keepdims=True)
    xc = xf - mu
    var = jnp.mean(xc * xc, axis=-1, keepdims=True)
    y = xc * lax.rsqrt(var + LN_EPS)
    if gain is not None:
        y = y * gain.astype(jnp.float32) + bias.astype(jnp.float32)
    return y


def multiscale_pool(u_ext, n_hist, w_pool_l, scale_l):
    b, l, c = u_ext.shape
    t = l - n_hist
    uf = u_ext.astype(jnp.float32)
    cs = jnp.concatenate([jnp.zeros((b, 1, c), jnp.float32), jnp.cumsum(uf, axis=1)], axis=1)
    end = n_hist + 1 + jnp.arange(t)
    cs_end = cs[:, n_hist + 1:]
    cur = uf[:, n_hist:]
    groups = []
    for g, w in enumerate(POOL_WINDOWS):
        sl = slice(g * POOL_GROUP_WIDTH, (g + 1) * POOL_GROUP_WIDTH)
        start = jnp.maximum(end - w, 0)
        count = (end - start).astype(jnp.float32)[None, :, None]
        win_sum = cs_end[..., sl] - jnp.take(cs[..., sl], start, axis=1)
        groups.append(win_sum / count - cur[..., sl])
    pooled = jnp.stack(groups, axis=2)
    mixed = jnp.einsum("btgc,gcd->btgd", pooled, w_pool_l.astype(jnp.float32)).reshape(b, t, c)
    return mixed * scale_l.astype(jnp.float32)


def stick_breaking_block(q_blk, q_pos, k, v, k_pos):
    z = jnp.einsum("bqhd,bkhd->bhqk", q_blk.astype(jnp.float32), k.astype(jnp.float32)) * SB_SCALE
    mask = k_pos[None, :] < q_pos[:, None]
    log_1mb = jnp.where(mask, jax.nn.log_sigmoid(-z), 0.0)
    log_a = jax.nn.log_sigmoid(z) + lax.cumsum(log_1mb, axis=3, reverse=True) - log_1mb
    a = jnp.where(mask, jnp.exp(log_a), 0.0)
    return jnp.einsum("bhqk,bkhd->bqhd", a, v.astype(jnp.float32))


def stick_breaking_prompt(q, k, v):
    b, s, h, d = q.shape
    nb = s // QBLOCK
    qb = q.reshape(b, nb, QBLOCK, h, d).transpose(1, 0, 2, 3, 4)
    k_pos = jnp.arange(s)

    def one_block(args):
        blk, i = args
        q_pos = i * QBLOCK + jnp.arange(QBLOCK)
        return stick_breaking_block(blk, q_pos, k, v, k_pos)

    out = lax.map(one_block, (qb, jnp.arange(nb)))
    return out.transpose(1, 0, 2, 3, 4).reshape(b, s, h, d)


def stick_breaking_sample(q, k_all, v_all, n_past):
    t = q.shape[1]
    k_pos = jnp.arange(k_all.shape[1])
    q_pos = n_past + jnp.arange(t)
    return stick_breaking_block(q, q_pos, k_all, v_all, k_pos)


def token_mixers(h, w_in_l, w_pool_l, pool_scale_l, hist):
    b, t, _ = h.shape
    proj = h @ w_in_l
    u = proj[..., :POOL_WIDTH]
    o = POOL_WIDTH
    q = proj[..., o:o + SB_WIDTH].reshape(b, t, SB_HEADS, SB_HEAD_DIM)
    k = proj[..., o + SB_WIDTH:o + 2 * SB_WIDTH].reshape(b, t, SB_HEADS, SB_HEAD_DIM)
    v = proj[..., o + 2 * SB_WIDTH:o + 3 * SB_WIDTH].reshape(b, t, SB_HEADS, SB_HEAD_DIM)
    if hist is None:
        u_ext, n_hist = u, 0
        sb = stick_breaking_prompt(q, k, v)
    else:
        cache_k_l, cache_v_l, pool_l = hist
        u_ext, n_hist = jnp.concatenate([pool_l, u], axis=1), POOL_HIST
        sb = stick_breaking_sample(q, jnp.concatenate([cache_k_l, k], axis=1),
                                   jnp.concatenate([cache_v_l, v], axis=1), cache_k_l.shape[1])
    pool_out = multiscale_pool(u_ext, n_hist, w_pool_l, pool_scale_l)
    mixed = jnp.concatenate([pool_out, sb.reshape(b, t, SB_WIDTH)], axis=-1).astype(h.dtype)
    return mixed, k, v, u_ext[:, -POOL_HIST:]


def run_trunk(x, c, hist, w_ada, b_ada, w_in, w_pool, pool_scale, w_o,
              ln1_g, ln1_b, w_gu, w_down, ln2_g, ln2_b):
    dt = x.dtype
    ks, vs, ps = [], [], []
    cond = jax.nn.silu(c)
    for l in range(DEPTH):
        mod = (cond @ w_ada[l] + b_ada[l]).astype(jnp.float32)[:, None, :]
        sh1, sc1, g1, sh2, sc2, g2 = jnp.split(mod, 6, axis=-1)
        h = (layer_norm(x) * (1.0 + sc1) + sh1).astype(dt)
        layer_hist = None if hist is None else (hist[0][l], hist[1][l], hist[2][l])
        mixed, k, v, pool_state = token_mixers(h, w_in[l], w_pool[l], pool_scale[l], layer_hist)
        mix_out = (mixed @ w_o[l]).astype(jnp.float32)
        x = layer_norm(DEEPNORM_ALPHA * x.astype(jnp.float32) + g1 * mix_out, ln1_g[l], ln1_b[l]).astype(dt)
        h = (layer_norm(x) * (1.0 + sc2) + sh2).astype(dt)
        gu = h @ w_gu[l]
        ff = ((jax.nn.silu(gu[..., :D_FF]) * gu[..., D_FF:]) @ w_down[l]).astype(jnp.float32)
        x = layer_norm(DEEPNORM_ALPHA * x.astype(jnp.float32) + g2 * ff, ln2_g[l], ln2_b[l]).astype(dt)
        ks.append(k)
        vs.append(v)
        ps.append(pool_state)
    return x, jnp.stack(ks), jnp.stack(vs), jnp.stack(ps)


def setup_inputs(seed: int = 0) -> dict:
    key = jax.random.key(seed)
    kk = jax.random.split(key, 20)
    f32 = jnp.float32
    nrm = lambda k, shape: jax.random.normal(k, shape, f32)
    v_col = jnp.concatenate([jnp.ones((POOL_WIDTH + 2 * SB_WIDTH,), f32),
                             jnp.full((SB_WIDTH,), DEEPNORM_BETA, f32)])
    return {
        "x_prompt": nrm(kk[0], (BATCH, SEQ, D_MODEL)),
        "x_sample": nrm(kk[1], (DEC_BATCH, DEC_SEQ, D_MODEL)),
        "cache_k": nrm(kk[2], (DEPTH, DEC_BATCH, PAST_LEN, SB_HEADS, SB_HEAD_DIM)),
        "cache_v": DEEPNORM_BETA * nrm(kk[3], (DEPTH, DEC_BATCH, PAST_LEN, SB_HEADS, SB_HEAD_DIM)),
        "state_pool": nrm(kk[4], (DEPTH, DEC_BATCH, POOL_HIST, POOL_WIDTH)),
        "c_prompt": nrm(kk[5], (BATCH, D_MODEL)),
        "c_sample": nrm(kk[6], (DEC_BATCH, D_MODEL)),
        "w_ada": 0.5 * D_MODEL ** -0.5 * nrm(kk[7], (DEPTH, D_MODEL, 6 * D_MODEL)),
        "b_ada": 0.02 * nrm(kk[8], (DEPTH, 6 * D_MODEL)),
        "w_in": D_MODEL ** -0.5 * nrm(kk[9], (DEPTH, D_MODEL, POOL_WIDTH + 3 * SB_WIDTH)) * v_col,
        "w_pool": POOL_GROUP_WIDTH ** -0.5 * nrm(kk[10], (DEPTH, POOL_GROUPS, POOL_GROUP_WIDTH, POOL_GROUP_WIDTH)),
        "pool_scale": 1.0 + 0.1 * nrm(kk[11], (DEPTH, POOL_WIDTH)),
        "w_o": DEEPNORM_BETA * MIX_WIDTH ** -0.5 * nrm(kk[12], (DEPTH, MIX_WIDTH, D_MODEL)),
        "ln1_g": 1.0 + 0.02 * nrm(kk[13], (DEPTH, D_MODEL)),
        "ln1_b": 0.02 * nrm(kk[14], (DEPTH, D_MODEL)),
        "w_gu": D_MODEL ** -0.5 * nrm(kk[15], (DEPTH, D_MODEL, 2 * D_FF)),
        "w_down": DEEPNORM_BETA * D_FF ** -0.5 * nrm(kk[16], (DEPTH, D_FF, D_MODEL)),
        "ln2_g": 1.0 + 0.02 * nrm(kk[17], (DEPTH, D_MODEL)),
        "ln2_b": 0.02 * nrm(kk[18], (DEPTH, D_MODEL)),
    }


def reference(x_prompt, x_sample, cache_k, cache_v, state_pool, c_prompt, c_sample,
              w_ada, b_ada, w_in, w_pool, pool_scale, w_o, ln1_g, ln1_b,
              w_gu, w_down, ln2_g, ln2_b):
    y_prompt, new_k_prompt, new_v_prompt, new_pool_prompt = run_trunk(
        x_prompt, c_prompt, None, w_ada, b_ada, w_in, w_pool, pool_scale, w_o,
        ln1_g, ln1_b, w_gu, w_down, ln2_g, ln2_b)
    y_sample, new_k_sample, new_v_sample, new_pool_sample = run_trunk(
        x_sample, c_sample, (cache_k, cache_v, state_pool), w_ada, b_ada, w_in, w_pool, pool_scale, w_o,
        ln1_g, ln1_b, w_gu, w_down, ln2_g, ln2_b)
    return (y_prompt, y_sample, new_k_prompt, new_v_prompt, new_pool_prompt,
            new_k_sample, new_v_sample, new_pool_sample)
```

```python
import functools
import math

import jax
import jax.numpy as jnp
from jax import lax
from jax.experimental import pallas as pl
from jax.experimental.pallas import tpu as pltpu

F32 = jnp.float32
BF16 = jnp.bfloat16

POOL_WINDOWS = (2, 4, 8, 16)
POOL_GROUP_WIDTH = 128
POOL_WIDTH = POOL_GROUP_WIDTH * len(POOL_WINDOWS)
POOL_HIST = max(POOL_WINDOWS) - 1
HALO_ROWS = 16
SB_HEAD_DIM = 64
SB_SCALE = 1.0 / math.sqrt(SB_HEAD_DIM)
HEAD_PAIR_WIDTH = 2 * SB_HEAD_DIM
LN_EPS = 1e-5

V7X_VMEM_LIMIT_BYTES = 56 * 1024 * 1024

ROW_TILE = 512
ATTN_TILE = 256
CACHE_KEYS_PER_STEP = 1024
ADA_COL_TILE = 1536


def _dot(a, b):
    return jnp.dot(a, b, preferred_element_type=F32)


def _dot_nt(a, b):
    return lax.dot_general(a, b, (((1,), (1,)), ((), ())), preferred_element_type=F32)


def _split_bf16(x):
    hi = x.astype(BF16)
    lo = (x - hi.astype(F32)).astype(BF16)
    return hi, lo


def _layer_norm(x):
    mu = jnp.mean(x, axis=-1, keepdims=True)
    xc = x - mu
    var = jnp.mean(xc * xc, axis=-1, keepdims=True)
    return xc * lax.rsqrt(var + LN_EPS)


def _silu(x):
    return x / (1.0 + jnp.exp(-x))


def _neg_softplus(z):
    return -(jnp.maximum(z, 0.0) + jnp.log(1.0 + jnp.exp(-jnp.abs(z))))


def _suffix_ones(n):
    r = lax.broadcasted_iota(jnp.int32, (n, n), 0)
    c = lax.broadcasted_iota(jnp.int32, (n, n), 1)
    return jnp.where(r >= c, 1.0, 0.0).astype(BF16)


def _adaln_kernel(c_ref, w_ref, b_ref, o_ref):
    cond = _silu(c_ref[...])
    c_hi, c_lo = _split_bf16(cond)
    w_hi, w_lo = _split_bf16(w_ref[...])
    o_ref[...] = _dot(c_hi, w_hi) + _dot(c_hi, w_lo) + _dot(c_lo, w_hi) + b_ref[...]


def _adaln(c_all, w_ada, b_ada):
    depth, d_model, n_out = w_ada.shape
    rows = c_all.shape[0]
    tn = ADA_COL_TILE
    return pl.pallas_call(
        _adaln_kernel,
        grid=(depth, n_out // tn),
        in_specs=[
            pl.BlockSpec((rows, d_model), lambda l, j: (0, 0)),
            pl.BlockSpec((None, d_model, tn), lambda l, j: (l, 0, j)),
            pl.BlockSpec((None, 1, tn), lambda l, j: (l, 0, j)),
        ],
        out_specs=pl.BlockSpec((None, rows, tn), lambda l, j: (l, 0, j)),
        out_shape=jax.ShapeDtypeStruct((depth, rows, n_out), F32),
        compiler_params=pltpu.CompilerParams(
            dimension_semantics=("arbitrary", "arbitrary"), vmem_limit_bytes=V7X_VMEM_LIMIT_BYTES),
        name="adaln",
    )(c_all, w_ada, b_ada.reshape(depth, 1, n_out))


def _inproj_kernel(x_ref, mod_ref, w_ref, u_ref, k_ref, v_ref, qb_ref, kb_ref, vb_ref, *, d_model):
    x = x_ref[...]
    mod = mod_ref[...]
    shift = mod[..., 0:d_model]
    scale = mod[..., d_model:2 * d_model]
    h = (_layer_norm(x) * (1.0 + scale) + shift).astype(BF16)
    h = h.reshape(-1, d_model)
    proj = _dot(h, w_ref[...])
    sb = (proj.shape[-1] - POOL_WIDTH) // 3
    o = POOL_WIDTH
    u = proj[:, :o]
    q = proj[:, o:o + sb]
    k = proj[:, o + sb:o + 2 * sb]
    v = proj[:, o + 2 * sb:o + 3 * sb]
    u_ref[...] = u
    k_ref[...] = k
    v_ref[...] = v
    qb_ref[...] = (q * SB_SCALE).astype(BF16)
    kb_ref[...] = k.astype(BF16)
    vb_ref[...] = v.astype(BF16)


def _inproj(x, mod, w_in_bf, layer):
    d_model = x.shape[-1]
    n_out = w_in_bf.shape[-1]
    sb = (n_out - POOL_WIDTH) // 3
    if x.ndim == 2:
        rows = x.shape[0]
        tm = ROW_TILE
        grid = (rows // tm,)
        x_spec = pl.BlockSpec((tm, d_model), lambda i: (i, 0))
        mod_spec = pl.BlockSpec(mod.shape, lambda i: (0, 0))
    else:
        rows = x.shape[0] * x.shape[1]
        tm = rows
        grid = (1,)
        x_spec = pl.BlockSpec(x.shape, lambda i: (0, 0, 0))
        mod_spec = pl.BlockSpec(mod.shape, lambda i: (0, 0, 0))
    row_spec = lambda w: pl.BlockSpec((tm, w), lambda i: (i, 0))
    return pl.pallas_call(
        functools.partial(_inproj_kernel, d_model=d_model),
        grid=grid,
        in_specs=[x_spec, mod_spec,
                  pl.BlockSpec((None, d_model, n_out), lambda i: (layer, 0, 0))],
        out_specs=[row_spec(POOL_WIDTH), row_spec(sb), row_spec(sb), row_spec(sb), row_spec(sb), row_spec(sb)],
        out_shape=[jax.ShapeDtypeStruct((rows, POOL_WIDTH), F32),
                   jax.ShapeDtypeStruct((rows, sb), F32),
                   jax.ShapeDtypeStruct((rows, sb), F32),
                   jax.ShapeDtypeStruct((rows, sb), BF16),
                   jax.ShapeDtypeStruct((rows, sb), BF16),
                   jax.ShapeDtypeStruct((rows, sb), BF16)],
        compiler_params=pltpu.CompilerParams(
            dimension_semantics=("arbitrary",), vmem_limit_bytes=V7X_VMEM_LIMIT_BYTES),
        name="inproj",
    )(x, mod, w_in_bf)


def _sb_block(z, v_bf, suffix, carry, mask):
    log_1mb = _neg_softplus(z)
    if mask is not None:
        log_1mb = jnp.where(mask, log_1mb, 0.0)
    hi, lo = _split_bf16(log_1mb)
    cs = _dot(hi, suffix) + _dot(lo, suffix)
    p = jnp.exp(z + cs + carry)
    if mask is not None:
        p = jnp.where(mask, p, 0.0)
    return _dot(p.astype(BF16), v_bf), jnp.sum(log_1mb, axis=-1, keepdims=True)


def _attn_prompt_kernel(q_ref, k_ref, v_ref, o_ref, *, tile):
    i = pl.program_id(1)
    qp = q_ref[...]
    lane = lax.broadcasted_iota(jnp.int32, qp.shape, 1)
    zero = jnp.zeros_like(qp)
    q_heads = (jnp.where(lane < SB_HEAD_DIM, qp, zero), jnp.where(lane >= SB_HEAD_DIM, qp, zero))
    suffix = _suffix_ones(tile)
    row = lax.broadcasted_iota(jnp.int32, (tile, tile), 0)
    col = lax.broadcasted_iota(jnp.int32, (tile, tile), 1)
    causal = col < row

    def block(j, state, mask):
        start = pl.multiple_of(j * tile, tile)
        k_blk = k_ref[pl.ds(start, tile), :]
        v_blk = v_ref[pl.ds(start, tile), :]
        new = []
        for qh, (carry, acc) in zip(q_heads, state):
            pv, blk_sum = _sb_block(_dot_nt(qh, k_blk), v_blk, suffix, carry, mask)
            new.append((carry + blk_sum, acc + pv))
        return tuple(new)

    init = tuple((jnp.zeros((tile, 1), F32), jnp.zeros((tile, HEAD_PAIR_WIDTH), F32)) for _ in q_heads)
    state = block(i, init, causal)
    state = lax.fori_loop(0, i, lambda t, s: block(i - 1 - t, s, None), state)
    out_lane = lax.broadcasted_iota(jnp.int32, (tile, HEAD_PAIR_WIDTH), 1)
    o_ref[...] = jnp.where(out_lane < SB_HEAD_DIM, state[0][1], state[1][1])


def _attn_prompt(q_bf, k_bf, v_bf):
    seq, width = q_bf.shape
    tile = ATTN_TILE
    pairs = width // HEAD_PAIR_WIDTH
    return pl.pallas_call(
        functools.partial(_attn_prompt_kernel, tile=tile),
        grid=(pairs, seq // tile),
        in_specs=[
            pl.BlockSpec((tile, HEAD_PAIR_WIDTH), lambda p, i: (i, p)),
            pl.BlockSpec((seq, HEAD_PAIR_WIDTH), lambda p, i: (0, p)),
            pl.BlockSpec((seq, HEAD_PAIR_WIDTH), lambda p, i: (0, p)),
        ],
        out_specs=pl.BlockSpec((tile, HEAD_PAIR_WIDTH), lambda p, i: (i, p)),
        out_shape=jax.ShapeDtypeStruct((seq, width), F32),
        compiler_params=pltpu.CompilerParams(
            dimension_semantics=("arbitrary", "arbitrary"), vmem_limit_bytes=V7X_VMEM_LIMIT_BYTES),
        name="attn_prompt",
    )(q_bf, k_bf, v_bf)


def _attn_sample_kernel(q_ref, kn_ref, vn_ref, ck_ref, cv_ref, o_ref, qs_ref, carry_ref, acc_ref,
                        *, t_new, heads, sub):
    j = pl.program_id(1)
    n_steps = pl.num_programs(1)
    width = heads * SB_HEAD_DIM
    stacked = heads * t_new

    @pl.when(j == 0)
    def _():
        q = q_ref[...]
        lane = lax.broadcasted_iota(jnp.int32, q.shape, 1)
        zero = jnp.zeros_like(q)
        for h in range(heads):
            in_head = (lane >= h * SB_HEAD_DIM) & (lane < (h + 1) * SB_HEAD_DIM)
            qs_ref[h * t_new:(h + 1) * t_new, :] = jnp.where(in_head, q, zero)
        row = lax.broadcasted_iota(jnp.int32, (stacked, t_new), 0)
        col = lax.broadcasted_iota(jnp.int32, (stacked, t_new), 1)
        causal = col < (row % t_new)
        z = _dot_nt(qs_ref[...], kn_ref[...])
        pv, blk_sum = _sb_block(z, vn_ref[...], _suffix_ones(t_new), jnp.zeros((stacked, 1), F32), causal)
        acc_ref[...] = pv
        carry_ref[...] = blk_sum

    suffix = _suffix_ones(sub)
    n_sub = ck_ref.shape[0] // sub
    qs = qs_ref[...]
    for s in reversed(range(n_sub)):
        k_blk = ck_ref[s * sub:(s + 1) * sub, :].astype(BF16)
        v_blk = cv_ref[s * sub:(s + 1) * sub, :].astype(BF16)
        pv, blk_sum = _sb_block(_dot_nt(qs, k_blk), v_blk, suffix, carry_ref[...], None)
        acc_ref[...] += pv
        carry_ref[...] += blk_sum

    @pl.when(j == n_steps - 1)
    def _():
        lane = lax.broadcasted_iota(jnp.int32, (t_new, width), 1)
        out = jnp.zeros((t_new, width), F32)
        for h in range(heads):
            in_head = (lane >= h * SB_HEAD_DIM) & (lane < (h + 1) * SB_HEAD_DIM)
            out = jnp.where(in_head, acc_ref[h * t_new:(h + 1) * t_new, :], out)
        o_ref[...] = out


def _attn_sample(q_bf, k_bf, v_bf, cache_k, cache_v, layer):
    b, t_new, width = q_bf.shape
    past = cache_k.shape[2]
    heads = width // SB_HEAD_DIM
    keys = CACHE_KEYS_PER_STEP
    n_steps = past // keys
    new_spec = pl.BlockSpec((None, t_new, width), lambda bi, j: (bi, 0, 0))
    cache_spec = pl.BlockSpec((None, None, keys, width), lambda bi, j: (layer, bi, n_steps - 1 - j, 0))
    return pl.pallas_call(
        functools.partial(_attn_sample_kernel, t_new=t_new, heads=heads, sub=ATTN_TILE),
        grid=(b, n_steps),
        in_specs=[new_spec, new_spec, new_spec, cache_spec, cache_spec],
        out_specs=pl.BlockSpec((None, t_new, width), lambda bi, j: (bi, 0, 0)),
        out_shape=jax.ShapeDtypeStruct((b, t_new, width), F32),
        scratch_shapes=[pltpu.VMEM((heads * t_new, width), BF16),
                        pltpu.VMEM((heads * t_new, 1), F32),
                        pltpu.VMEM((heads * t_new, width), F32)],
        compiler_params=pltpu.CompilerParams(
            dimension_semantics=("arbitrary", "arbitrary"), vmem_limit_bytes=V7X_VMEM_LIMIT_BYTES),
        name="attn_sample",
    )(q_bf, k_bf, v_bf, cache_k, cache_v)


def _trailing_window_sums(u_ext, axis):
    n = u_ext.shape[axis]
    t = n - HALO_ROWS
    sl = lambda a, lo, hi: lax.slice_in_dim(a, lo, hi, axis=axis)
    cols = lambda a, g: a[..., g * POOL_GROUP_WIDTH:]
    s2 = sl(u_ext, 1, n) + sl(u_ext, 0, n - 1)
    s2b = cols(s2, 1)
    s4 = sl(s2b, 2, n - 1) + sl(s2b, 0, n - 3)
    s4b = s4[..., POOL_GROUP_WIDTH:]
    s8 = sl(s4b, 4, n - 3) + sl(s4b, 0, n - 7)
    s8b = s8[..., POOL_GROUP_WIDTH:]
    s16 = sl(s8b, 8, n - 7) + sl(s8b, 0, n - 15)
    first = HALO_ROWS
    return (sl(s2, first - 1, first - 1 + t)[..., :POOL_GROUP_WIDTH],
            sl(s4, first - 3, first - 3 + t)[..., :POOL_GROUP_WIDTH],
            sl(s8, first - 7, first - 7 + t)[..., :POOL_GROUP_WIDTH],
            sl(s16, first - 15, first - 15 + t))


def _post_kernel(x_ref, u_ref, halo_ref, s_ref, mod_ref, wpool_ref, pscale_ref, wo_ref, g1_ref, b1_ref,
                 wgu_ref, wdown_ref, g2_ref, b2_ref, o_ref, *, d_model, d_ff, alpha, ff_chunks, has_state):
    x = x_ref[...]
    mod = mod_ref[...]
    gate1 = mod[..., 2 * d_model:3 * d_model]
    shift2 = mod[..., 3 * d_model:4 * d_model]
    scale2 = mod[..., 4 * d_model:5 * d_model]
    gate2 = mod[..., 5 * d_model:6 * d_model]

    u = u_ref[...]
    halo = halo_ref[...]
    row_axis = u.ndim - 2
    rows = u.shape[row_axis]
    if has_state:
        inv_counts = [1.0 / w for w in POOL_WINDOWS]
    else:
        first = pl.program_id(0) == 0
        halo = jnp.where(first, jnp.zeros_like(halo), halo)
        t = pl.program_id(0) * rows + lax.broadcasted_iota(jnp.int32, (rows, 1), 0)
        avail = (t + 1).astype(F32)
        inv_counts = [1.0 / jnp.minimum(avail, float(w)) for w in POOL_WINDOWS]
    sums = _trailing_window_sums(jnp.concatenate([halo, u], axis=row_axis), row_axis)
    pool_out = []
    for g, (win_sum, inv) in enumerate(zip(sums, inv_counts)):
        cur = u[..., g * POOL_GROUP_WIDTH:(g + 1) * POOL_GROUP_WIDTH]
        pooled = (win_sum * inv - cur).reshape(-1, POOL_GROUP_WIDTH)
        pool_out.append(_dot(pooled.astype(BF16), wpool_ref[g]))
    pool_out = jnp.concatenate(pool_out, axis=-1) * pscale_ref[...]

    sb_out = s_ref[...].reshape(-1, s_ref.shape[-1])
    mix = (_dot(pool_out.astype(BF16), wo_ref[0:POOL_WIDTH, :])
           + _dot(sb_out.astype(BF16), wo_ref[POOL_WIDTH:, :]))
    mix = mix.reshape(x.shape)
    x1 = _layer_norm(alpha * x + gate1 * mix) * g1_ref[...] + b1_ref[...]

    h = (_layer_norm(x1) * (1.0 + scale2) + shift2).astype(BF16).reshape(-1, d_model)
    ff = None
    for lo, hi in ff_chunks:
        gate = _dot(h, wgu_ref[:, lo:hi])
        up = _dot(h, wgu_ref[:, d_ff + lo:d_ff + hi])
        part = _dot((_silu(gate) * up).astype(BF16), wdown_ref[lo:hi, :])
        ff = part if ff is None else ff + part
    ff = ff.reshape(x.shape)
    o_ref[...] = _layer_norm(alpha * x1 + gate2 * ff) * g2_ref[...] + b2_ref[...]


def _ff_chunks(d_ff):
    mxu_cols = 256
    tiles = d_ff // mxu_cols
    assert tiles * mxu_cols == d_ff
    split = (tiles + 1) // 2 * mxu_cols
    return ((0, split), (split, d_ff))


def _post(x, u, halo_src, s, mod, layer, w_pool_bf, pool_scale, w_o_bf, ln1_g, ln1_b, w_gu_bf, w_down_bf,
          ln2_g, ln2_b, alpha):
    d_model = x.shape[-1]
    d_ff = w_down_bf.shape[1]
    depth = w_o_bf.shape[0]
    has_state = x.ndim == 3
    if has_state:
        grid = (1,)
        full3 = lambda a: pl.BlockSpec(a.shape, lambda i: (0, 0, 0))
        x_spec, u_spec, halo_spec, s_spec, mod_spec, out_spec = full3(x), full3(u), full3(halo_src), full3(s), full3(mod), full3(x)
    else:
        rows = x.shape[0]
        tm = ROW_TILE
        grid = (rows // tm,)
        halo_blocks = tm // HALO_ROWS
        x_spec = pl.BlockSpec((tm, d_model), lambda i: (i, 0))
        u_spec = pl.BlockSpec((tm, POOL_WIDTH), lambda i: (i, 0))
        halo_spec = pl.BlockSpec((HALO_ROWS, POOL_WIDTH), lambda i: (jnp.maximum(i * halo_blocks - 1, 0), 0))
        s_spec = pl.BlockSpec((tm, s.shape[-1]), lambda i: (i, 0))
        mod_spec = pl.BlockSpec(mod.shape, lambda i: (0, 0))
        out_spec = x_spec
    const = pl.Buffered(1)

    def layer_spec(a):
        nd = a.ndim - 1
        return pl.BlockSpec((None,) + a.shape[1:], lambda i: (layer,) + (0,) * nd, pipeline_mode=const)

    vec = lambda a: a.reshape(depth, 1, a.shape[-1])
    return pl.pallas_call(
        functools.partial(_post_kernel, d_model=d_model, d_ff=d_ff, alpha=alpha, ff_chunks=_ff_chunks(d_ff),
                          has_state=has_state),
        grid=grid,
        in_specs=[x_spec, u_spec, halo_spec, s_spec, mod_spec,
                  layer_spec(w_pool_bf), layer_spec(vec(pool_scale)), layer_spec(w_o_bf),
                  layer_spec(vec(ln1_g)), layer_spec(vec(ln1_b)),
                  layer_spec(w_gu_bf), layer_spec(w_down_bf),
                  layer_spec(vec(ln2_g)), layer_spec(vec(ln2_b))],
        out_specs=out_spec,
        out_shape=jax.ShapeDtypeStruct(x.shape, F32),
        compiler_params=pltpu.CompilerParams(
            dimension_semantics=("arbitrary",), vmem_limit_bytes=V7X_VMEM_LIMIT_BYTES),
        name="post",
    )(x, u, halo_src, s, mod, w_pool_bf, vec(pool_scale), w_o_bf, vec(ln1_g), vec(ln1_b),
      w_gu_bf, w_down_bf, vec(ln2_g), vec(ln2_b))


def kernel(x_prompt, x_sample, cache_k, cache_v, state_pool, c_prompt, c_sample, w_ada, b_ada, w_in, w_pool,
           pool_scale, w_o, ln1_g, ln1_b, w_gu, w_down, ln2_g, ln2_b):
    depth, d_model, _ = w_ada.shape
    batch, seq, _ = x_prompt.shape
    dec_batch, dec_seq, _ = x_sample.shape
    past = cache_k.shape[2]
    heads, head_dim = cache_k.shape[3], cache_k.shape[4]
    sb_width = heads * head_dim
    assert batch == 1 and head_dim == SB_HEAD_DIM and state_pool.shape[2] == POOL_HIST
    assert seq % ROW_TILE == 0 and past % CACHE_KEYS_PER_STEP == 0 and dec_seq >= POOL_HIST
    alpha = (2 * depth) ** 0.25

    n_cond = batch + dec_batch
    c_all = jnp.concatenate([c_prompt, c_sample], axis=0)
    c_all = jnp.pad(c_all, ((0, -n_cond % 16), (0, 0)))
    mod = _adaln(c_all, w_ada, b_ada)

    w_in_bf, w_pool_bf, w_o_bf = w_in.astype(BF16), w_pool.astype(BF16), w_o.astype(BF16)
    w_gu_bf, w_down_bf = w_gu.astype(BF16), w_down.astype(BF16)
    cache_k = cache_k.reshape(depth, dec_batch, past, sb_width)
    cache_v = cache_v.reshape(depth, dec_batch, past, sb_width)
    hist = jnp.pad(state_pool, ((0, 0), (0, 0), (HALO_ROWS - POOL_HIST, 0), (0, 0)))

    xp = x_prompt.reshape(seq, d_model)
    xs = x_sample
    k_p, v_p, pool_p, k_s, v_s, pool_s = [], [], [], [], [], []
    for l in range(depth):
        post_w = (l, w_pool_bf, pool_scale, w_o_bf, ln1_g, ln1_b, w_gu_bf, w_down_bf, ln2_g, ln2_b, alpha)

        mod_p = mod[l, 0:batch]
        u, k, v, qb, kb, vb = _inproj(xp, mod_p, w_in_bf, l)
        s = _attn_prompt(qb, kb, vb)
        xp = _post(xp, u, u, s, mod_p, *post_w)
        k_p.append(k.reshape(batch, seq, heads, head_dim))
        v_p.append(v.reshape(batch, seq, heads, head_dim))
        pool_p.append(u[seq - POOL_HIST:].reshape(batch, POOL_HIST, POOL_WIDTH))

        mod_s = mod[l, batch:n_cond].reshape(dec_batch, 1, -1)
        u, k, v, qb, kb, vb = _inproj(xs, mod_s, w_in_bf, l)
        to3 = lambda a: a.reshape(dec_batch, dec_seq, a.shape[-1])
        s = _attn_sample(to3(qb), to3(kb), to3(vb), cache_k, cache_v, l)
        u3 = to3(u)
        xs = _post(xs, u3, hist[l], s, mod_s, *post_w)
        k_s.append(k.reshape(dec_batch, dec_seq, heads, head_dim))
        v_s.append(v.reshape(dec_batch, dec_seq, heads, head_dim))
        pool_s.append(u3[:, dec_seq - POOL_HIST:])

    return (xp.reshape(batch, seq, d_model), xs, jnp.stack(k_p), jnp.stack(v_p), jnp.stack(pool_p),
            jnp.stack(k_s), jnp.stack(v_s), jnp.stack(pool_s))
```

```python
import functools
import math

import jax
import jax.numpy as jnp
from jax import lax
from jax.experimental import pallas as pl
from jax.experimental.pallas import tpu as pltpu

F32 = jnp.float32
BF16 = jnp.bfloat16

POOL_WINDOWS = (2, 4, 8, 16)
POOL_GROUP_WIDTH = 128
POOL_WIDTH = POOL_GROUP_WIDTH * len(POOL_WINDOWS)
POOL_HIST = max(POOL_WINDOWS) - 1
HALO_ROWS = 16
SB_HEAD_DIM = 64
SB_SCALE = 1.0 / math.sqrt(SB_HEAD_DIM)
LOG2_E = 1.0 / math.log(2.0)
HEAD_PAIR_WIDTH = 2 * SB_HEAD_DIM
LN_EPS = 1e-5
F32_UNDERFLOW_BITS = 151.0

V7X_VMEM_LIMIT_BYTES = 56 * 1024 * 1024

ROW_TILE = 512
ATTN_TILE = 256
CACHE_KEYS_PER_STEP = 1024
ADA_COL_TILE = 1536


def _dot(a, b):
    return jnp.dot(a, b, preferred_element_type=F32)


def _dot_nt(a, b):
    return lax.dot_general(a, b, (((1,), (1,)), ((), ())), preferred_element_type=F32)


def _split_bf16(x):
    hi = x.astype(BF16)
    lo = (x - hi.astype(F32)).astype(BF16)
    return hi, lo


def _layer_norm(x):
    mu = jnp.mean(x, axis=-1, keepdims=True)
    xc = x - mu
    var = jnp.mean(xc * xc, axis=-1, keepdims=True)
    return xc * lax.rsqrt(var + LN_EPS)


def _silu(x):
    return x / (1.0 + jnp.exp(-x))


def _softplus2(z2):
    return jnp.maximum(z2, 0.0) + LOG2_E * jnp.log(1.0 + jnp.exp2(-jnp.abs(z2)))


def _suffix_ones(n):
    r = lax.broadcasted_iota(jnp.int32, (n, n), 0)
    c = lax.broadcasted_iota(jnp.int32, (n, n), 1)
    return jnp.where(r >= c, 1.0, 0.0).astype(BF16)


def _adaln_kernel(c_ref, w_ref, b_ref, o_ref):
    cond = _silu(c_ref[...])
    c_hi, c_lo = _split_bf16(cond)
    w_hi, w_lo = _split_bf16(w_ref[...])
    o_ref[...] = _dot(c_hi, w_hi) + _dot(c_hi, w_lo) + _dot(c_lo, w_hi) + b_ref[...]


def _adaln(c_all, w_ada, b_ada):
    depth, d_model, n_out = w_ada.shape
    rows = c_all.shape[0]
    tn = ADA_COL_TILE
    return pl.pallas_call(
        _adaln_kernel,
        grid=(depth, n_out // tn),
        in_specs=[
            pl.BlockSpec((rows, d_model), lambda l, j: (0, 0)),
            pl.BlockSpec((None, d_model, tn), lambda l, j: (l, 0, j)),
            pl.BlockSpec((None, 1, tn), lambda l, j: (l, 0, j)),
        ],
        out_specs=pl.BlockSpec((None, rows, tn), lambda l, j: (l, 0, j)),
        out_shape=jax.ShapeDtypeStruct((depth, rows, n_out), F32),
        compiler_params=pltpu.CompilerParams(
            dimension_semantics=("arbitrary", "arbitrary"), vmem_limit_bytes=V7X_VMEM_LIMIT_BYTES),
        name="adaln",
    )(c_all, w_ada, b_ada.reshape(depth, 1, n_out))


def _inproj_kernel(x_ref, mod_ref, w_ref, u_ref, k_ref, v_ref, qb_ref, kb_ref, vb_ref, *, d_model):
    x = x_ref[...]
    mod = mod_ref[...]
    shift = mod[..., 0:d_model]
    scale = mod[..., d_model:2 * d_model]
    h = (_layer_norm(x) * (1.0 + scale) + shift).astype(BF16)
    h = h.reshape(-1, d_model)
    proj = _dot(h, w_ref[...])
    sb = (proj.shape[-1] - POOL_WIDTH) // 3
    o = POOL_WIDTH
    u = proj[:, :o]
    q = proj[:, o:o + sb]
    k = proj[:, o + sb:o + 2 * sb]
    v = proj[:, o + 2 * sb:o + 3 * sb]
    u_ref[...] = u
    k_ref[...] = k
    v_ref[...] = v
    qb_ref[...] = (q * (SB_SCALE * LOG2_E)).astype(BF16)
    kb_ref[...] = k.astype(BF16)
    vb_ref[...] = v.astype(BF16)


def _inproj(x, mod, w_in_bf, layer):
    d_model = x.shape[-1]
    n_out = w_in_bf.shape[-1]
    sb = (n_out - POOL_WIDTH) // 3
    if x.ndim == 2:
        rows = x.shape[0]
        tm = ROW_TILE
        grid = (rows // tm,)
        x_spec = pl.BlockSpec((tm, d_model), lambda i: (i, 0))
        mod_spec = pl.BlockSpec(mod.shape, lambda i: (0, 0))
    else:
        rows = x.shape[0] * x.shape[1]
        tm = rows
        grid = (1,)
        x_spec = pl.BlockSpec(x.shape, lambda i: (0, 0, 0))
        mod_spec = pl.BlockSpec(mod.shape, lambda i: (0, 0, 0))
    row_spec = lambda w: pl.BlockSpec((tm, w), lambda i: (i, 0))
    return pl.pallas_call(
        functools.partial(_inproj_kernel, d_model=d_model),
        grid=grid,
        in_specs=[x_spec, mod_spec,
                  pl.BlockSpec((None, d_model, n_out), lambda i: (layer, 0, 0))],
        out_specs=[row_spec(POOL_WIDTH), row_spec(sb), row_spec(sb), row_spec(sb), row_spec(sb), row_spec(sb)],
        out_shape=[jax.ShapeDtypeStruct((rows, POOL_WIDTH), F32),
                   jax.ShapeDtypeStruct((rows, sb), F32),
                   jax.ShapeDtypeStruct((rows, sb), F32),
                   jax.ShapeDtypeStruct((rows, sb), BF16),
                   jax.ShapeDtypeStruct((rows, sb), BF16),
                   jax.ShapeDtypeStruct((rows, sb), BF16)],
        compiler_params=pltpu.CompilerParams(
            dimension_semantics=("arbitrary",), vmem_limit_bytes=V7X_VMEM_LIMIT_BYTES),
        name="inproj",
    )(x, mod, w_in_bf)


def _sb_blocks(z2s, v_bfs, suffix, carries, mask):
    bits = [_softplus2(z2) for z2 in z2s]
    if mask is not None:
        bits = [jnp.where(mask, b, 0.0) for b in bits]
    parts = [_split_bf16(b) for b in bits]
    later = [_dot(hi, suffix) + _dot(lo, suffix) for hi, lo in parts]
    ps = [jnp.exp2(z2 - (l + c)) for z2, l, c in zip(z2s, later, carries)]
    if mask is not None:
        ps = [jnp.where(mask, p, 0.0) for p in ps]
    pvs = [_dot(p.astype(BF16), v) for p, v in zip(ps, v_bfs)]
    return pvs, [jnp.sum(b, axis=-1, keepdims=True) for b in bits]


def _sb_block(z2, v_bf, suffix, carry, mask):
    pvs, sums = _sb_blocks([z2], [v_bf], suffix, [carry], mask)
    return pvs[0], sums[0]


def _attn_prompt_kernel(q_ref, k_ref, v_ref, o_ref, carry_ref, acc_ref, *, tile, heads):
    i = pl.program_id(0)
    lane = lax.broadcasted_iota(jnp.int32, (tile, HEAD_PAIR_WIDTH), 1)
    low = lane < SB_HEAD_DIM
    q_heads = []
    for h in range(heads):
        qp = q_ref[:, (h // 2) * HEAD_PAIR_WIDTH:(h // 2 + 1) * HEAD_PAIR_WIDTH]
        q_heads.append(jnp.where(low if h % 2 == 0 else ~low, qp, jnp.zeros_like(qp)))
    suffix = _suffix_ones(tile)
    row = lax.broadcasted_iota(jnp.int32, (tile, tile), 0)
    col = lax.broadcasted_iota(jnp.int32, (tile, tile), 1)
    causal = col < row

    def block(j, mask, first):
        start = pl.multiple_of(j * tile, tile)
        pair_cols = lambda h: slice((h // 2) * HEAD_PAIR_WIDTH, (h // 2 + 1) * HEAD_PAIR_WIDTH)
        z2s = [_dot_nt(q_heads[h], k_ref[pl.ds(start, tile), pair_cols(h)]) for h in range(heads)]
        v_blks = [v_ref[pl.ds(start, tile), pair_cols(h)] for h in range(heads)]
        carries = [jnp.zeros((tile, 1), F32) if first else carry_ref[h] for h in range(heads)]
        pvs, blk_bits = _sb_blocks(z2s, v_blks, suffix, carries, mask)
        least = None
        for h in range(heads):
            carry = carries[h] + blk_bits[h]
            carry_ref[h] = carry
            acc_ref[h] = pvs[h] if first else acc_ref[h] + pvs[h]
            least = carry if least is None else jnp.minimum(least, carry)
        return jnp.min(least)

    least = block(i, causal, True)

    def more(state):
        j, least = state
        return (j >= 0) & (least < F32_UNDERFLOW_BITS)

    lax.while_loop(more, lambda state: (state[0] - 1, block(state[0], None, False)), (i - 1, least))
    for p in range(heads // 2):
        o_ref[:, p * HEAD_PAIR_WIDTH:(p + 1) * HEAD_PAIR_WIDTH] = jnp.where(low, acc_ref[2 * p], acc_ref[2 * p + 1])


def _attn_prompt(q_bf, k_bf, v_bf):
    seq, width = q_bf.shape
    tile = ATTN_TILE
    heads = width // SB_HEAD_DIM
    resident = lambda: pl.BlockSpec((seq, width), lambda i: (0, 0), pipeline_mode=pl.Buffered(1))
    return pl.pallas_call(
        functools.partial(_attn_prompt_kernel, tile=tile, heads=heads),
        grid=(seq // tile,),
        in_specs=[pl.BlockSpec((tile, width), lambda i: (i, 0)), resident(), resident()],
        out_specs=pl.BlockSpec((tile, width), lambda i: (i, 0)),
        out_shape=jax.ShapeDtypeStruct((seq, width), F32),
        scratch_shapes=[pltpu.VMEM((heads, tile, 1), F32),
                        pltpu.VMEM((heads, tile, HEAD_PAIR_WIDTH), F32)],
        compiler_params=pltpu.CompilerParams(
            dimension_semantics=("arbitrary",), vmem_limit_bytes=V7X_VMEM_LIMIT_BYTES),
        name="attn_prompt",
    )(q_bf, k_bf, v_bf)


def _attn_sample_kernel(q_ref, kn_ref, vn_ref, ck_ref, cv_ref, o_ref, qs_ref, carry_ref, acc_ref,
                        *, t_new, heads, sub):
    j = pl.program_id(1)
    n_steps = pl.num_programs(1)
    width = heads * SB_HEAD_DIM
    stacked = heads * t_new

    @pl.when(j == 0)
    def _():
        q = q_ref[...]
        lane = lax.broadcasted_iota(jnp.int32, q.shape, 1)
        zero = jnp.zeros_like(q)
        for h in range(heads):
            in_head = (lane >= h * SB_HEAD_DIM) & (lane < (h + 1) * SB_HEAD_DIM)
            qs_ref[h * t_new:(h + 1) * t_new, :] = jnp.where(in_head, q, zero)
        row = lax.broadcasted_iota(jnp.int32, (stacked, t_new), 0)
        col = lax.broadcasted_iota(jnp.int32, (stacked, t_new), 1)
        causal = col < (row % t_new)
        z = _dot_nt(qs_ref[...], kn_ref[...])
        pv, blk_sum = _sb_block(z, vn_ref[...], _suffix_ones(t_new), jnp.zeros((stacked, 1), F32), causal)
        acc_ref[...] = pv
        carry_ref[...] = blk_sum

    suffix = _suffix_ones(sub)
    n_sub = ck_ref.shape[0] // sub
    qs = qs_ref[...]
    for s in reversed(range(n_sub)):
        k_blk = ck_ref[s * sub:(s + 1) * sub, :].astype(BF16)
        v_blk = cv_ref[s * sub:(s + 1) * sub, :].astype(BF16)
        pv, blk_sum = _sb_block(_dot_nt(qs, k_blk), v_blk, suffix, carry_ref[...], None)
        acc_ref[...] += pv
        carry_ref[...] += blk_sum

    @pl.when(j == n_steps - 1)
    def _():
        lane = lax.broadcasted_iota(jnp.int32, (t_new, width), 1)
        out = jnp.zeros((t_new, width), F32)
        for h in range(heads):
            in_head = (lane >= h * SB_HEAD_DIM) & (lane < (h + 1) * SB_HEAD_DIM)
            out = jnp.where(in_head, acc_ref[h * t_new:(h + 1) * t_new, :], out)
        o_ref[...] = out


def _attn_sample(q_bf, k_bf, v_bf, cache_k, cache_v, layer):
    b, t_new, width = q_bf.shape
    past = cache_k.shape[2]
    heads = width // SB_HEAD_DIM
    keys = CACHE_KEYS_PER_STEP
    n_steps = past // keys
    new_spec = pl.BlockSpec((None, t_new, width), lambda bi, j: (bi, 0, 0))
    cache_spec = pl.BlockSpec((None, None, keys, width), lambda bi, j: (layer, bi, n_steps - 1 - j, 0))
    return pl.pallas_call(
        functools.partial(_attn_sample_kernel, t_new=t_new, heads=heads, sub=ATTN_TILE),
        grid=(b, n_steps),
        in_specs=[new_spec, new_spec, new_spec, cache_spec, cache_spec],
        out_specs=pl.BlockSpec((None, t_new, width), lambda bi, j: (bi, 0, 0)),
        out_shape=jax.ShapeDtypeStruct((b, t_new, width), F32),
        scratch_shapes=[pltpu.VMEM((heads * t_new, width), BF16),
                        pltpu.VMEM((heads * t_new, 1), F32),
                        pltpu.VMEM((heads * t_new, width), F32)],
        compiler_params=pltpu.CompilerParams(
            dimension_semantics=("arbitrary", "arbitrary"), vmem_limit_bytes=V7X_VMEM_LIMIT_BYTES),
        name="attn_sample",
    )(q_bf, k_bf, v_bf, cache_k, cache_v)


def _trailing_window_sums(u_ext, axis):
    n = u_ext.shape[axis]
    t = n - HALO_ROWS
    sl = lambda a, lo, hi: lax.slice_in_dim(a, lo, hi, axis=axis)
    cols = lambda a, g: a[..., g * POOL_GROUP_WIDTH:]
    s2 = sl(u_ext, 1, n) + sl(u_ext, 0, n - 1)
    s2b = cols(s2, 1)
    s4 = sl(s2b, 2, n - 1) + sl(s2b, 0, n - 3)
    s4b = s4[..., POOL_GROUP_WIDTH:]
    s8 = sl(s4b, 4, n - 3) + sl(s4b, 0, n - 7)
    s8b = s8[..., POOL_GROUP_WIDTH:]
    s16 = sl(s8b, 8, n - 7) + sl(s8b, 0, n - 15)
    first = HALO_ROWS
    return (sl(s2, first - 1, first - 1 + t)[..., :POOL_GROUP_WIDTH],
            sl(s4, first - 3, first - 3 + t)[..., :POOL_GROUP_WIDTH],
            sl(s8, first - 7, first - 7 + t)[..., :POOL_GROUP_WIDTH],
            sl(s16, first - 15, first - 15 + t))


def _post_kernel(x_ref, u_ref, halo_ref, s_ref, mod_ref, wpool_ref, pscale_ref, wo_ref, g1_ref, b1_ref,
                 wgu_ref, wdown_ref, g2_ref, b2_ref, o_ref, *, d_model, d_ff, alpha, ff_chunks, has_state):
    x = x_ref[...]
    mod = mod_ref[...]
    gate1 = mod[..., 2 * d_model:3 * d_model]
    shift2 = mod[..., 3 * d_model:4 * d_model]
    scale2 = mod[..., 4 * d_model:5 * d_model]
    gate2 = mod[..., 5 * d_model:6 * d_model]

    u = u_ref[...]
    halo = halo_ref[...]
    row_axis = u.ndim - 2
    rows = u.shape[row_axis]
    if has_state:
        inv_counts = [1.0 / w for w in POOL_WINDOWS]
    else:
        first = pl.program_id(0) == 0
        halo = jnp.where(first, jnp.zeros_like(halo), halo)
        t = pl.program_id(0) * rows + lax.broadcasted_iota(jnp.int32, (rows, 1), 0)
        avail = (t + 1).astype(F32)
        inv_counts = [1.0 / jnp.minimum(avail, float(w)) for w in POOL_WINDOWS]
    sums = _trailing_window_sums(jnp.concatenate([halo, u], axis=row_axis), row_axis)
    pool_out = []
    for g, (win_sum, inv) in enumerate(zip(sums, inv_counts)):
        cur = u[..., g * POOL_GROUP_WIDTH:(g + 1) * POOL_GROUP_WIDTH]
        pooled = (win_sum * inv - cur).reshape(-1, POOL_GROUP_WIDTH)
        pool_out.append(_dot(pooled.astype(BF16), wpool_ref[g]))
    pool_out = jnp.concatenate(pool_out, axis=-1) * pscale_ref[...]

    sb_out = s_ref[...].reshape(-1, s_ref.shape[-1])
    mix = (_dot(pool_out.astype(BF16), wo_ref[0:POOL_WIDTH, :])
           + _dot(sb_out.astype(BF16), wo_ref[POOL_WIDTH:, :]))
    mix = mix.reshape(x.shape)
    x1 = _layer_norm(alpha * x + gate1 * mix) * g1_ref[...] + b1_ref[...]

    h = (_layer_norm(x1) * (1.0 + scale2) + shift2).astype(BF16).reshape(-1, d_model)
    ff = None
    for lo, hi in ff_chunks:
        gate = _dot(h, wgu_ref[:, lo:hi])
        up = _dot(h, wgu_ref[:, d_ff + lo:d_ff + hi])
        part = _dot((_silu(gate) * up).astype(BF16), wdown_ref[lo:hi, :])
        ff = part if ff is None else ff + part
    ff = ff.reshape(x.shape)
    o_ref[...] = _layer_norm(alpha * x1 + gate2 * ff) * g2_ref[...] + b2_ref[...]


def _ff_chunks(d_ff):
    mxu_cols = 256
    tiles = d_ff // mxu_cols
    assert tiles * mxu_cols == d_ff
    split = (tiles + 1) // 2 * mxu_cols
    return ((0, split), (split, d_ff))


def _post(x, u, halo_src, s, mod, layer, w_pool_bf, pool_scale, w_o_bf, ln1_g, ln1_b, w_gu_bf, w_down_bf,
          ln2_g, ln2_b, alpha):
    d_model = x.shape[-1]
    d_ff = w_down_bf.shape[1]
    depth = w_o_bf.shape[0]
    has_state = x.ndim == 3
    if has_state:
        grid = (1,)
        full3 = lambda a: pl.BlockSpec(a.shape, lambda i: (0, 0, 0))
        x_spec, u_spec, halo_spec, s_spec, mod_spec, out_spec = full3(x), full3(u), full3(halo_src), full3(s), full3(mod), full3(x)
    else:
        rows = x.shape[0]
        tm = ROW_TILE
        grid = (rows // tm,)
        halo_blocks = tm // HALO_ROWS
        x_spec = pl.BlockSpec((tm, d_model), lambda i: (i, 0))
        u_spec = pl.BlockSpec((tm, POOL_WIDTH), lambda i: (i, 0))
        halo_spec = pl.BlockSpec((HALO_ROWS, POOL_WIDTH), lambda i: (jnp.maximum(i * halo_blocks - 1, 0), 0))
        s_spec = pl.BlockSpec((tm, s.shape[-1]), lambda i: (i, 0))
        mod_spec = pl.BlockSpec(mod.shape, lambda i: (0, 0))
        out_spec = x_spec
    const = pl.Buffered(1)

    def layer_spec(a):
        nd = a.ndim - 1
        return pl.BlockSpec((None,) + a.shape[1:], lambda i: (layer,) + (0,) * nd, pipeline_mode=const)

    vec = lambda a: a.reshape(depth, 1, a.shape[-1])
    return pl.pallas_call(
        functools.partial(_post_kernel, d_model=d_model, d_ff=d_ff, alpha=alpha, ff_chunks=_ff_chunks(d_ff),
                          has_state=has_state),
        grid=grid,
        in_specs=[x_spec, u_spec, halo_spec, s_spec, mod_spec,
                  layer_spec(w_pool_bf), layer_spec(vec(pool_scale)), layer_spec(w_o_bf),
                  layer_spec(vec(ln1_g)), layer_spec(vec(ln1_b)),
                  layer_spec(w_gu_bf), layer_spec(w_down_bf),
                  layer_spec(vec(ln2_g)), layer_spec(vec(ln2_b))],
        out_specs=out_spec,
        out_shape=jax.ShapeDtypeStruct(x.shape, F32),
        compiler_params=pltpu.CompilerParams(
            dimension_semantics=("arbitrary",), vmem_limit_bytes=V7X_VMEM_LIMIT_BYTES),
        name="post",
    )(x, u, halo_src, s, mod, w_pool_bf, vec(pool_scale), w_o_bf, vec(ln1_g), vec(ln1_b),
      w_gu_bf, w_down_bf, vec(ln2_g), vec(ln2_b))


def kernel(x_prompt, x_sample, cache_k, cache_v, state_pool, c_prompt, c_sample, w_ada, b_ada, w_in, w_pool,
           pool_scale, w_o, ln1_g, ln1_b, w_gu, w_down, ln2_g, ln2_b):
    depth, d_model, _ = w_ada.shape
    batch, seq, _ = x_prompt.shape
    dec_batch, dec_seq, _ = x_sample.shape
    past = cache_k.shape[2]
    heads, head_dim = cache_k.shape[3], cache_k.shape[4]
    sb_width = heads * head_dim
    assert batch == 1 and head_dim == SB_HEAD_DIM and state_pool.shape[2] == POOL_HIST
    assert seq % ROW_TILE == 0 and past % CACHE_KEYS_PER_STEP == 0 and dec_seq >= POOL_HIST
    alpha = (2 * depth) ** 0.25

    n_cond = batch + dec_batch
    c_all = jnp.concatenate([c_prompt, c_sample], axis=0)
    c_all = jnp.pad(c_all, ((0, -n_cond % 16), (0, 0)))
    mod = _adaln(c_all, w_ada, b_ada)

    w_in_bf, w_pool_bf, w_o_bf = w_in.astype(BF16), w_pool.astype(BF16), w_o.astype(BF16)
    w_gu_bf, w_down_bf = w_gu.astype(BF16), w_down.astype(BF16)
    cache_k = cache_k.reshape(depth, dec_batch, past, sb_width)
    cache_v = cache_v.reshape(depth, dec_batch, past, sb_width)
    hist = jnp.pad(state_pool, ((0, 0), (0, 0), (HALO_ROWS - POOL_HIST, 0), (0, 0)))

    xp = x_prompt.reshape(seq, d_model)
    xs = x_sample
    k_p, v_p, pool_p, k_s, v_s, pool_s = [], [], [], [], [], []
    for l in range(depth):
        post_w = (l, w_pool_bf, pool_scale, w_o_bf, ln1_g, ln1_b, w_gu_bf, w_down_bf, ln2_g, ln2_b, alpha)

        mod_p = mod[l, 0:batch]
        u, k, v, qb, kb, vb = _inproj(xp, mod_p, w_in_bf, l)
        s = _attn_prompt(qb, kb, vb)
        xp = _post(xp, u, u, s, mod_p, *post_w)
        k_p.append(k.reshape(batch, seq, heads, head_dim))
        v_p.append(v.reshape(batch, seq, heads, head_dim))
        pool_p.append(u[seq - POOL_HIST:].reshape(batch, POOL_HIST, POOL_WIDTH))

        mod_s = mod[l, batch:n_cond].reshape(dec_batch, 1, -1)
        u, k, v, qb, kb, vb = _inproj(xs, mod_s, w_in_bf, l)
        to3 = lambda a: a.reshape(dec_batch, dec_seq, a.shape[-1])
        s = _attn_sample(to3(qb), to3(kb), to3(vb), cache_k, cache_v, l)
        u3 = to3(u)
        xs = _post(xs, u3, hist[l], s, mod_s, *post_w)
        k_s.append(k.reshape(dec_batch, dec_seq, heads, head_dim))
        v_s.append(v.reshape(dec_batch, dec_seq, heads, head_dim))
        pool_s.append(u3[:, dec_seq - POOL_HIST:])

    return (xp.reshape(batch, seq, d_model), xs, jnp.stack(k_p), jnp.stack(v_p), jnp.stack(pool_p),
            jnp.stack(k_s), jnp.stack(v_s), jnp.stack(pool_s))
```

```python
import functools
import math

import jax
import jax.numpy as jnp
from jax import lax
from jax.experimental import pallas as pl
from jax.experimental.pallas import tpu as pltpu

F32 = jnp.float32
BF16 = jnp.bfloat16

POOL_WINDOWS = (2, 4, 8, 16)
POOL_GROUP_WIDTH = 128
POOL_WIDTH = POOL_GROUP_WIDTH * len(POOL_WINDOWS)
POOL_HIST = max(POOL_WINDOWS) - 1
HALO_ROWS = 16
SB_HEAD_DIM = 64
SB_SCALE = 1.0 / math.sqrt(SB_HEAD_DIM)
LOG2_E = 1.0 / math.log(2.0)
HEAD_PAIR_WIDTH = 2 * SB_HEAD_DIM
LN_EPS = 1e-5
F32_UNDERFLOW_BITS = 151.0

V7X_VMEM_LIMIT_BYTES = 56 * 1024 * 1024

ROW_TILE = 512
ATTN_TILE = 256
SAMPLE_CHUNK_TOKENS = 256
ADA_COL_TILE = 1536


def _dot(a, b):
    return jnp.dot(a, b, preferred_element_type=F32)


def _dot_nt(a, b):
    return lax.dot_general(a, b, (((1,), (1,)), ((), ())), preferred_element_type=F32)


def _split_bf16(x):
    hi = x.astype(BF16)
    lo = (x - hi.astype(F32)).astype(BF16)
    return hi, lo


def _layer_norm(x):
    mu = jnp.mean(x, axis=-1, keepdims=True)
    xc = x - mu
    var = jnp.mean(xc * xc, axis=-1, keepdims=True)
    return xc * lax.rsqrt(var + LN_EPS)


def _silu(x):
    return x / (1.0 + jnp.exp(-x))


def _softplus2(z2):
    return jnp.maximum(z2, 0.0) + LOG2_E * jnp.log(1.0 + jnp.exp2(-jnp.abs(z2)))


def _suffix_ones(n):
    r = lax.broadcasted_iota(jnp.int32, (n, n), 0)
    c = lax.broadcasted_iota(jnp.int32, (n, n), 1)
    return jnp.where(r >= c, 1.0, 0.0).astype(BF16)


def _adaln_kernel(c_ref, w_ref, b_ref, o_ref):
    cond = _silu(c_ref[...])
    c_hi, c_lo = _split_bf16(cond)
    w_hi, w_lo = _split_bf16(w_ref[...])
    o_ref[...] = _dot(c_hi, w_hi) + _dot(c_hi, w_lo) + _dot(c_lo, w_hi) + b_ref[...]


def _adaln(c_all, w_ada, b_ada):
    depth, d_model, n_out = w_ada.shape
    rows = c_all.shape[0]
    tn = ADA_COL_TILE
    return pl.pallas_call(
        _adaln_kernel,
        grid=(depth, n_out // tn),
        in_specs=[
            pl.BlockSpec((rows, d_model), lambda l, j: (0, 0)),
            pl.BlockSpec((None, d_model, tn), lambda l, j: (l, 0, j)),
            pl.BlockSpec((None, 1, tn), lambda l, j: (l, 0, j)),
        ],
        out_specs=pl.BlockSpec((None, rows, tn), lambda l, j: (l, 0, j)),
        out_shape=jax.ShapeDtypeStruct((depth, rows, n_out), F32),
        compiler_params=pltpu.CompilerParams(
            dimension_semantics=("arbitrary", "arbitrary"), vmem_limit_bytes=V7X_VMEM_LIMIT_BYTES),
        name="adaln",
    )(c_all, w_ada, b_ada.reshape(depth, 1, n_out))


def _inproj_kernel(x_ref, mod_ref, w_ref, u_ref, k_ref, v_ref, qb_ref, kb_ref, vb_ref, *, d_model):
    x = x_ref[...]
    mod = mod_ref[...]
    shift = mod[..., 0:d_model]
    scale = mod[..., d_model:2 * d_model]
    h = (_layer_norm(x) * (1.0 + scale) + shift).astype(BF16)
    h = h.reshape(-1, d_model)
    proj = _dot(h, w_ref[...])
    sb = (proj.shape[-1] - POOL_WIDTH) // 3
    o = POOL_WIDTH
    u = proj[:, :o]
    q = proj[:, o:o + sb]
    k = proj[:, o + sb:o + 2 * sb]
    v = proj[:, o + 2 * sb:o + 3 * sb]
    u_ref[...] = u
    k_ref[...] = k
    v_ref[...] = v
    qb_ref[...] = (q * (SB_SCALE * LOG2_E)).astype(BF16)
    kb_ref[...] = k.astype(BF16)
    vb_ref[...] = v.astype(BF16)


def _inproj(x, mod, w_in_bf, layer):
    d_model = x.shape[-1]
    n_out = w_in_bf.shape[-1]
    sb = (n_out - POOL_WIDTH) // 3
    if x.ndim == 2:
        rows = x.shape[0]
        tm = ROW_TILE
        grid = (rows // tm,)
        x_spec = pl.BlockSpec((tm, d_model), lambda i: (i, 0))
        mod_spec = pl.BlockSpec(mod.shape, lambda i: (0, 0))
    else:
        rows = x.shape[0] * x.shape[1]
        tm = rows
        grid = (1,)
        x_spec = pl.BlockSpec(x.shape, lambda i: (0, 0, 0))
        mod_spec = pl.BlockSpec(mod.shape, lambda i: (0, 0, 0))
    row_spec = lambda w: pl.BlockSpec((tm, w), lambda i: (i, 0))
    return pl.pallas_call(
        functools.partial(_inproj_kernel, d_model=d_model),
        grid=grid,
        in_specs=[x_spec, mod_spec,
                  pl.BlockSpec((None, d_model, n_out), lambda i: (layer, 0, 0))],
        out_specs=[row_spec(POOL_WIDTH), row_spec(sb), row_spec(sb), row_spec(sb), row_spec(sb), row_spec(sb)],
        out_shape=[jax.ShapeDtypeStruct((rows, POOL_WIDTH), F32),
                   jax.ShapeDtypeStruct((rows, sb), F32),
                   jax.ShapeDtypeStruct((rows, sb), F32),
                   jax.ShapeDtypeStruct((rows, sb), BF16),
                   jax.ShapeDtypeStruct((rows, sb), BF16),
                   jax.ShapeDtypeStruct((rows, sb), BF16)],
        compiler_params=pltpu.CompilerParams(
            dimension_semantics=("arbitrary",), vmem_limit_bytes=V7X_VMEM_LIMIT_BYTES),
        name="inproj",
    )(x, mod, w_in_bf)


def _sb_bits(z2s, suffix, mask):
    bits = [_softplus2(z2) for z2 in z2s]
    if mask is not None:
        bits = [jnp.where(mask, b, 0.0) for b in bits]
    parts = [_split_bf16(b) for b in bits]
    later = [_dot(hi, suffix) + _dot(lo, suffix) for hi, lo in parts]
    return later, [jnp.sum(b, axis=-1, keepdims=True) for b in bits]


def _sb_weighted(z2s, later, carries, v_bfs, mask, v_transposed=False):
    ps = [jnp.exp2(z2 - (l + c)) for z2, l, c in zip(z2s, later, carries)]
    if mask is not None:
        ps = [jnp.where(mask, p, 0.0) for p in ps]
    pv_dot = _dot_nt if v_transposed else _dot
    return [pv_dot(p.astype(BF16), v) for p, v in zip(ps, v_bfs)]


def _sb_blocks(z2s, v_bfs, suffix, carries, mask):
    later, sums = _sb_bits(z2s, suffix, mask)
    return _sb_weighted(z2s, later, carries, v_bfs, mask), sums


def _sb_chain(z2s, v_bfs, suffix, carry, mask, v_transposed=False):
    later, sums = _sb_bits(z2s, suffix, mask)
    carries = [carry]
    for block_sum in sums:
        carries.append(carries[-1] + block_sum)
    pvs = _sb_weighted(z2s, later, carries[:-1], v_bfs, mask, v_transposed)
    total = pvs[0]
    for pv in pvs[1:]:
        total = total + pv
    return total, carries[-1]


def _attn_prompt_kernel(q_ref, k_ref, v_ref, o_ref, carry_ref, acc_ref, *, tile, heads):
    i = pl.program_id(0)
    lane = lax.broadcasted_iota(jnp.int32, (tile, HEAD_PAIR_WIDTH), 1)
    low = lane < SB_HEAD_DIM
    q_heads = []
    for h in range(heads):
        qp = q_ref[:, (h // 2) * HEAD_PAIR_WIDTH:(h // 2 + 1) * HEAD_PAIR_WIDTH]
        q_heads.append(jnp.where(low if h % 2 == 0 else ~low, qp, jnp.zeros_like(qp)))
    suffix = _suffix_ones(tile)
    row = lax.broadcasted_iota(jnp.int32, (tile, tile), 0)
    col = lax.broadcasted_iota(jnp.int32, (tile, tile), 1)
    causal = col < row

    def block(j, mask, first):
        start = pl.multiple_of(j * tile, tile)
        pair_cols = lambda h: slice((h // 2) * HEAD_PAIR_WIDTH, (h // 2 + 1) * HEAD_PAIR_WIDTH)
        z2s = [_dot_nt(q_heads[h], k_ref[pl.ds(start, tile), pair_cols(h)]) for h in range(heads)]
        v_blks = [v_ref[pl.ds(start, tile), pair_cols(h)] for h in range(heads)]
        carries = [jnp.zeros((tile, 1), F32) if first else carry_ref[h] for h in range(heads)]
        pvs, blk_bits = _sb_blocks(z2s, v_blks, suffix, carries, mask)
        least = None
        for h in range(heads):
            carry = carries[h] + blk_bits[h]
            carry_ref[h] = carry
            acc_ref[h] = pvs[h] if first else acc_ref[h] + pvs[h]
            least = carry if least is None else jnp.minimum(least, carry)
        return jnp.min(least)

    least = block(i, causal, True)

    def more(state):
        j, least = state
        return (j >= 0) & (least < F32_UNDERFLOW_BITS)

    lax.while_loop(more, lambda state: (state[0] - 1, block(state[0], None, False)), (i - 1, least))
    for p in range(heads // 2):
        o_ref[:, p * HEAD_PAIR_WIDTH:(p + 1) * HEAD_PAIR_WIDTH] = jnp.where(low, acc_ref[2 * p], acc_ref[2 * p + 1])


def _attn_prompt(q_bf, k_bf, v_bf):
    seq, width = q_bf.shape
    tile = ATTN_TILE
    heads = width // SB_HEAD_DIM
    resident = lambda: pl.BlockSpec((seq, width), lambda i: (0, 0), pipeline_mode=pl.Buffered(1))
    return pl.pallas_call(
        functools.partial(_attn_prompt_kernel, tile=tile, heads=heads),
        grid=(seq // tile,),
        in_specs=[pl.BlockSpec((tile, width), lambda i: (i, 0)), resident(), resident()],
        out_specs=pl.BlockSpec((tile, width), lambda i: (i, 0)),
        out_shape=jax.ShapeDtypeStruct((seq, width), F32),
        scratch_shapes=[pltpu.VMEM((heads, tile, 1), F32),
                        pltpu.VMEM((heads, tile, HEAD_PAIR_WIDTH), F32)],
        compiler_params=pltpu.CompilerParams(
            dimension_semantics=("arbitrary",), vmem_limit_bytes=V7X_VMEM_LIMIT_BYTES),
        name="attn_prompt",
    )(q_bf, k_bf, v_bf)


def _attn_sample_kernel(q_ref, kn_ref, vn_ref, ck_hbm, cv_hbm, o_ref, kbuf, vbuf, sem, qs_ref, carry_ref, acc_ref,
                        *, layer, heads, past):
    n_batch, t_new, width = q_ref.shape
    q_rows = heads * t_new
    chunk = SAMPLE_CHUNK_TOKENS
    n_chunks = past // chunk
    lane = lax.broadcasted_iota(jnp.int32, (t_new, width), 1)
    in_head = [(lane >= h * SB_HEAD_DIM) & (lane < (h + 1) * SB_HEAD_DIM) for h in range(heads)]
    row = lax.broadcasted_iota(jnp.int32, (q_rows, t_new), 0)
    col = lax.broadcasted_iota(jnp.int32, (q_rows, t_new), 1)
    causal_new = col < (row % t_new)
    suffix_new = _suffix_ones(t_new)
    suffix = _suffix_ones(chunk)

    def chunk_copies(b, c, slot):
        tokens = pl.ds(pl.multiple_of(past - (c + 1) * chunk, chunk), chunk)
        return (pltpu.make_async_copy(ck_hbm.at[layer, b, :, :, tokens], kbuf.at[slot], sem.at[0, slot]),
                pltpu.make_async_copy(cv_hbm.at[layer, b, :, :, tokens], vbuf.at[slot], sem.at[1, slot]))

    def add_chunk(slot):
        k_t = kbuf[slot].reshape(width, chunk).astype(BF16)
        v_t = vbuf[slot].reshape(width, chunk).astype(BF16)
        pv, carry = _sb_chain([_dot(qs_ref[...], k_t)], [v_t], suffix, carry_ref[...], None, v_transposed=True)
        acc_ref[...] += pv
        carry_ref[...] = carry
        return jnp.min(carry)

    def one_batch(b, _):
        @pl.when(b + 1 < n_batch)
        def _():
            for cp in chunk_copies(b + 1, 0, (b + 1) % 2):
                cp.start()

        q = q_ref[b]
        for h in range(heads):
            qs_ref[h * t_new:(h + 1) * t_new, :] = jnp.where(in_head[h], q, jnp.zeros_like(q))
        pv, carry = _sb_chain([_dot_nt(qs_ref[...], kn_ref[b])], [vn_ref[b]], suffix_new,
                              jnp.zeros((q_rows, 1), F32), causal_new)
        acc_ref[...] = pv
        carry_ref[...] = carry
        for cp in chunk_copies(b, 0, b % 2):
            cp.wait()
        least = add_chunk(b % 2)

        def more(state):
            c, least = state
            return (c < n_chunks) & (least < F32_UNDERFLOW_BITS)

        def older_chunk(state):
            c, _ = state
            copies = chunk_copies(b, c, 2)
            for cp in copies:
                cp.start()
            for cp in copies:
                cp.wait()
            return c + 1, add_chunk(2)

        lax.while_loop(more, older_chunk, (1, least))
        out = jnp.zeros((t_new, width), F32)
        for h in range(heads):
            out = jnp.where(in_head[h], acc_ref[h * t_new:(h + 1) * t_new, :], out)
        o_ref[b] = out
        return 0

    for cp in chunk_copies(0, 0, 0):
        cp.start()
    lax.fori_loop(0, n_batch, one_batch, 0)


def _attn_sample(q_bf, k_bf, v_bf, cache_k_t, cache_v_t, layer):
    n_batch, t_new, width = q_bf.shape
    heads, head_dim, past = cache_k_t.shape[2:]
    q_rows = heads * t_new
    full = lambda a: pl.BlockSpec(a.shape, lambda i: (0,) * a.ndim)
    chunk_buffers = pltpu.VMEM((3, heads, head_dim, SAMPLE_CHUNK_TOKENS), F32)
    return pl.pallas_call(
        functools.partial(_attn_sample_kernel, layer=layer, heads=heads, past=past),
        grid=(1,),
        in_specs=[full(q_bf), full(k_bf), full(v_bf),
                  pl.BlockSpec(memory_space=pl.ANY), pl.BlockSpec(memory_space=pl.ANY)],
        out_specs=full(q_bf),
        out_shape=jax.ShapeDtypeStruct(q_bf.shape, F32),
        scratch_shapes=[chunk_buffers, chunk_buffers, pltpu.SemaphoreType.DMA((2, 3)),
                        pltpu.VMEM((q_rows, width), BF16),
                        pltpu.VMEM((q_rows, 1), F32),
                        pltpu.VMEM((q_rows, width), F32)],
        compiler_params=pltpu.CompilerParams(
            dimension_semantics=("arbitrary",), vmem_limit_bytes=V7X_VMEM_LIMIT_BYTES),
        name="attn_sample",
    )(q_bf, k_bf, v_bf, cache_k_t, cache_v_t)


def _trailing_window_sums(u_ext, axis):
    n = u_ext.shape[axis]
    t = n - HALO_ROWS
    sl = lambda a, lo, hi: lax.slice_in_dim(a, lo, hi, axis=axis)
    cols = lambda a, g: a[..., g * POOL_GROUP_WIDTH:]
    s2 = sl(u_ext, 1, n) + sl(u_ext, 0, n - 1)
    s2b = cols(s2, 1)
    s4 = sl(s2b, 2, n - 1) + sl(s2b, 0, n - 3)
    s4b = s4[..., POOL_GROUP_WIDTH:]
    s8 = sl(s4b, 4, n - 3) + sl(s4b, 0, n - 7)
    s8b = s8[..., POOL_GROUP_WIDTH:]
    s16 = sl(s8b, 8, n - 7) + sl(s8b, 0, n - 15)
    first = HALO_ROWS
    return (sl(s2, first - 1, first - 1 + t)[..., :POOL_GROUP_WIDTH],
            sl(s4, first - 3, first - 3 + t)[..., :POOL_GROUP_WIDTH],
            sl(s8, first - 7, first - 7 + t)[..., :POOL_GROUP_WIDTH],
            sl(s16, first - 15, first - 15 + t))


def _post_kernel(x_ref, u_ref, halo_ref, s_ref, mod_ref, wpool_ref, pscale_ref, wo_ref, g1_ref, b1_ref,
                 wgu_ref, wdown_ref, g2_ref, b2_ref, o_ref, *, d_model, d_ff, alpha, ff_chunks, has_state):
    x = x_ref[...]
    mod = mod_ref[...]
    gate1 = mod[..., 2 * d_model:3 * d_model]
    shift2 = mod[..., 3 * d_model:4 * d_model]
    scale2 = mod[..., 4 * d_model:5 * d_model]
    gate2 = mod[..., 5 * d_model:6 * d_model]

    u = u_ref[...]
    halo = halo_ref[...]
    row_axis = u.ndim - 2
    rows = u.shape[row_axis]
    if has_state:
        inv_counts = [1.0 / w for w in POOL_WINDOWS]
    else:
        first = pl.program_id(0) == 0
        halo = jnp.where(first, jnp.zeros_like(halo), halo)
        t = pl.program_id(0) * rows + lax.broadcasted_iota(jnp.int32, (rows, 1), 0)
        avail = (t + 1).astype(F32)
        inv_counts = [1.0 / jnp.minimum(avail, float(w)) for w in POOL_WINDOWS]
    sums = _trailing_window_sums(jnp.concatenate([halo, u], axis=row_axis), row_axis)
    pool_out = []
    for g, (win_sum, inv) in enumerate(zip(sums, inv_counts)):
        cur = u[..., g * POOL_GROUP_WIDTH:(g + 1) * POOL_GROUP_WIDTH]
        pooled = (win_sum * inv - cur).reshape(-1, POOL_GROUP_WIDTH)
        pool_out.append(_dot(pooled.astype(BF16), wpool_ref[g]))
    pool_out = jnp.concatenate(pool_out, axis=-1) * pscale_ref[...]

    sb_out = s_ref[...].reshape(-1, s_ref.shape[-1])
    mix = (_dot(pool_out.astype(BF16), wo_ref[0:POOL_WIDTH, :])
           + _dot(sb_out.astype(BF16), wo_ref[POOL_WIDTH:, :]))
    mix = mix.reshape(x.shape)
    x1 = _layer_norm(alpha * x + gate1 * mix) * g1_ref[...] + b1_ref[...]

    h = (_layer_norm(x1) * (1.0 + scale2) + shift2).astype(BF16).reshape(-1, d_model)
    ff = None
    for lo, hi in ff_chunks:
        gate = _dot(h, wgu_ref[:, lo:hi])
        up = _dot(h, wgu_ref[:, d_ff + lo:d_ff + hi])
        part = _dot((_silu(gate) * up).astype(BF16), wdown_ref[lo:hi, :])
        ff = part if ff is None else ff + part
    ff = ff.reshape(x.shape)
    o_ref[...] = _layer_norm(alpha * x1 + gate2 * ff) * g2_ref[...] + b2_ref[...]


def _ff_chunks(d_ff):
    mxu_cols = 256
    tiles = d_ff // mxu_cols
    assert tiles * mxu_cols == d_ff
    split = (tiles + 1) // 2 * mxu_cols
    return ((0, split), (split, d_ff))


def _post(x, u, halo_src, s, mod, layer, w_pool_bf, pool_scale, w_o_bf, ln1_g, ln1_b, w_gu_bf, w_down_bf,
          ln2_g, ln2_b, alpha):
    d_model = x.shape[-1]
    d_ff = w_down_bf.shape[1]
    depth = w_o_bf.shape[0]
    has_state = x.ndim == 3
    if has_state:
        grid = (1,)
        full3 = lambda a: pl.BlockSpec(a.shape, lambda i: (0, 0, 0))
        x_spec, u_spec, halo_spec, s_spec, mod_spec, out_spec = full3(x), full3(u), full3(halo_src), full3(s), full3(mod), full3(x)
    else:
        rows = x.shape[0]
        tm = ROW_TILE
        grid = (rows // tm,)
        halo_blocks = tm // HALO_ROWS
        x_spec = pl.BlockSpec((tm, d_model), lambda i: (i, 0))
        u_spec = pl.BlockSpec((tm, POOL_WIDTH), lambda i: (i, 0))
        halo_spec = pl.BlockSpec((HALO_ROWS, POOL_WIDTH), lambda i: (jnp.maximum(i * halo_blocks - 1, 0), 0))
        s_spec = pl.BlockSpec((tm, s.shape[-1]), lambda i: (i, 0))
        mod_spec = pl.BlockSpec(mod.shape, lambda i: (0, 0))
        out_spec = x_spec
    const = pl.Buffered(1)

    def layer_spec(a):
        nd = a.ndim - 1
        return pl.BlockSpec((None,) + a.shape[1:], lambda i: (layer,) + (0,) * nd, pipeline_mode=const)

    vec = lambda a: a.reshape(depth, 1, a.shape[-1])
    return pl.pallas_call(
        functools.partial(_post_kernel, d_model=d_model, d_ff=d_ff, alpha=alpha, ff_chunks=_ff_chunks(d_ff),
                          has_state=has_state),
        grid=grid,
        in_specs=[x_spec, u_spec, halo_spec, s_spec, mod_spec,
                  layer_spec(w_pool_bf), layer_spec(vec(pool_scale)), layer_spec(w_o_bf),
                  layer_spec(vec(ln1_g)), layer_spec(vec(ln1_b)),
                  layer_spec(w_gu_bf), layer_spec(w_down_bf),
                  layer_spec(vec(ln2_g)), layer_spec(vec(ln2_b))],
        out_specs=out_spec,
        out_shape=jax.ShapeDtypeStruct(x.shape, F32),
        compiler_params=pltpu.CompilerParams(
            dimension_semantics=("arbitrary",), vmem_limit_bytes=V7X_VMEM_LIMIT_BYTES),
        name="post",
    )(x, u, halo_src, s, mod, w_pool_bf, vec(pool_scale), w_o_bf, vec(ln1_g), vec(ln1_b),
      w_gu_bf, w_down_bf, vec(ln2_g), vec(ln2_b))


def kernel(x_prompt, x_sample, cache_k, cache_v, state_pool, c_prompt, c_sample, w_ada, b_ada, w_in, w_pool,
           pool_scale, w_o, ln1_g, ln1_b, w_gu, w_down, ln2_g, ln2_b):
    depth, d_model, _ = w_ada.shape
    batch, seq, _ = x_prompt.shape
    dec_batch, dec_seq, _ = x_sample.shape
    past = cache_k.shape[2]
    heads, head_dim = cache_k.shape[3], cache_k.shape[4]
    assert batch == 1 and head_dim == SB_HEAD_DIM and state_pool.shape[2] == POOL_HIST
    assert seq % ROW_TILE == 0 and past % SAMPLE_CHUNK_TOKENS == 0 and dec_seq >= POOL_HIST
    alpha = (2 * depth) ** 0.25

    n_cond = batch + dec_batch
    c_all = jnp.concatenate([c_prompt, c_sample], axis=0)
    c_all = jnp.pad(c_all, ((0, -n_cond % 16), (0, 0)))
    mod = _adaln(c_all, w_ada, b_ada)

    w_in_bf, w_pool_bf, w_o_bf = w_in.astype(BF16), w_pool.astype(BF16), w_o.astype(BF16)
    w_gu_bf, w_down_bf = w_gu.astype(BF16), w_down.astype(BF16)
    cache_k_t = cache_k.transpose(0, 1, 3, 4, 2)
    cache_v_t = cache_v.transpose(0, 1, 3, 4, 2)
    hist = jnp.pad(state_pool, ((0, 0), (0, 0), (HALO_ROWS - POOL_HIST, 0), (0, 0)))

    xp = x_prompt.reshape(seq, d_model)
    xs = x_sample
    k_p, v_p, pool_p, k_s, v_s, pool_s = [], [], [], [], [], []
    for l in range(depth):
        post_w = (l, w_pool_bf, pool_scale, w_o_bf, ln1_g, ln1_b, w_gu_bf, w_down_bf, ln2_g, ln2_b, alpha)

        mod_p = mod[l, 0:batch]
        u, k, v, qb, kb, vb = _inproj(xp, mod_p, w_in_bf, l)
        s = _attn_prompt(qb, kb, vb)
        xp = _post(xp, u, u, s, mod_p, *post_w)
        k_p.append(k.reshape(batch, seq, heads, head_dim))
        v_p.append(v.reshape(batch, seq, heads, head_dim))
        pool_p.append(u[seq - POOL_HIST:].reshape(batch, POOL_HIST, POOL_WIDTH))

        mod_s = mod[l, batch:n_cond].reshape(dec_batch, 1, -1)
        u, k, v, qb, kb, vb = _inproj(xs, mod_s, w_in_bf, l)
        to3 = lambda a: a.reshape(dec_batch, dec_seq, a.shape[-1])
        s = _attn_sample(to3(qb), to3(kb), to3(vb), cache_k_t, cache_v_t, l)
        u3 = to3(u)
        xs = _post(xs, u3, hist[l], s, mod_s, *post_w)
        k_s.append(k.reshape(dec_batch, dec_seq, heads, head_dim))
        v_s.append(v.reshape(dec_batch, dec_seq, heads, head_dim))
        pool_s.append(u3[:, dec_seq - POOL_HIST:])

    return (xp.reshape(batch, seq, d_model), xs, jnp.stack(k_p), jnp.stack(v_p), jnp.stack(pool_p),
            jnp.stack(k_s), jnp.stack(v_s), jnp.stack(pool_s))
```

```python
import functools
import math

import jax
import jax.numpy as jnp
from jax import lax
from jax.experimental import pallas as pl
from jax.experimental.pallas import tpu as pltpu

F32 = jnp.float32
BF16 = jnp.bfloat16

POOL_WINDOWS = (2, 4, 8, 16)
POOL_GROUP_WIDTH = 128
POOL_WIDTH = POOL_GROUP_WIDTH * len(POOL_WINDOWS)
POOL_HIST = max(POOL_WINDOWS) - 1
HALO_ROWS = 16
SB_HEAD_DIM = 64
SB_SCALE = 1.0 / math.sqrt(SB_HEAD_DIM)
LOG2_E = 1.0 / math.log(2.0)
HEAD_PAIR_WIDTH = 2 * SB_HEAD_DIM
LN_EPS = 1e-5
F32_UNDERFLOW_BITS = 151.0

V7X_VMEM_LIMIT_BYTES = 56 * 1024 * 1024

ROW_TILE = 512
POST_ROW_PARTS = 2
ATTN_TILE = 256
SAMPLE_CHUNK_TOKENS = 256
ADA_COL_TILE = 1536


def _dot(a, b):
    return jnp.dot(a, b, preferred_element_type=F32)


def _dot_nt(a, b):
    return lax.dot_general(a, b, (((1,), (1,)), ((), ())), preferred_element_type=F32)


def _split_bf16(x):
    hi = x.astype(BF16)
    lo = (x - hi.astype(F32)).astype(BF16)
    return hi, lo


def _layer_norm(x):
    mu = jnp.mean(x, axis=-1, keepdims=True)
    xc = x - mu
    var = jnp.mean(xc * xc, axis=-1, keepdims=True)
    return xc * lax.rsqrt(var + LN_EPS)


def _silu(x):
    return x / (1.0 + jnp.exp(-x))


def _softplus2(z2):
    return jnp.maximum(z2, 0.0) + LOG2_E * jnp.log(1.0 + jnp.exp2(-jnp.abs(z2)))


def _suffix_ones(n):
    r = lax.broadcasted_iota(jnp.int32, (n, n), 0)
    c = lax.broadcasted_iota(jnp.int32, (n, n), 1)
    return jnp.where(r >= c, 1.0, 0.0).astype(BF16)


def _adaln_kernel(c_ref, w_ref, b_ref, o_ref):
    cond = _silu(c_ref[...])
    c_hi, c_lo = _split_bf16(cond)
    w_hi, w_lo = _split_bf16(w_ref[...])
    o_ref[...] = _dot(c_hi, w_hi) + _dot(c_hi, w_lo) + _dot(c_lo, w_hi) + b_ref[...]


def _adaln(c_all, w_ada, b_ada):
    depth, d_model, n_out = w_ada.shape
    rows = c_all.shape[0]
    tn = ADA_COL_TILE
    return pl.pallas_call(
        _adaln_kernel,
        grid=(depth, n_out // tn),
        in_specs=[
            pl.BlockSpec((rows, d_model), lambda l, j: (0, 0)),
            pl.BlockSpec((None, d_model, tn), lambda l, j: (l, 0, j)),
            pl.BlockSpec((None, 1, tn), lambda l, j: (l, 0, j)),
        ],
        out_specs=pl.BlockSpec((None, rows, tn), lambda l, j: (l, 0, j)),
        out_shape=jax.ShapeDtypeStruct((depth, rows, n_out), F32),
        compiler_params=pltpu.CompilerParams(
            dimension_semantics=("arbitrary", "arbitrary"), vmem_limit_bytes=V7X_VMEM_LIMIT_BYTES),
        name="adaln",
    )(c_all, w_ada, b_ada.reshape(depth, 1, n_out))


def _inproj_kernel(x_ref, mod_ref, w_ref, u_ref, k_ref, v_ref, qb_ref, kb_ref, vb_ref, *, d_model):
    x = x_ref[...]
    mod = mod_ref[...]
    shift = mod[..., 0:d_model]
    scale = mod[..., d_model:2 * d_model]
    h = (_layer_norm(x) * (1.0 + scale) + shift).astype(BF16)
    h = h.reshape(-1, d_model)
    proj = _dot(h, w_ref[...])
    sb = (proj.shape[-1] - POOL_WIDTH) // 3
    o = POOL_WIDTH
    u = proj[:, :o]
    q = proj[:, o:o + sb]
    k = proj[:, o + sb:o + 2 * sb]
    v = proj[:, o + 2 * sb:o + 3 * sb]
    u_ref[...] = u
    k_ref[...] = k
    v_ref[...] = v
    qb_ref[...] = (q * (SB_SCALE * LOG2_E)).astype(BF16)
    kb_ref[...] = k.astype(BF16)
    vb_ref[...] = v.astype(BF16)


def _inproj(x, mod, w_in_bf, layer):
    d_model = x.shape[-1]
    n_out = w_in_bf.shape[-1]
    sb = (n_out - POOL_WIDTH) // 3
    rows = x.shape[0] * x.shape[1]
    row_spec = lambda w: pl.BlockSpec((rows, w), lambda i: (0, 0))
    return pl.pallas_call(
        functools.partial(_inproj_kernel, d_model=d_model),
        grid=(1,),
        in_specs=[pl.BlockSpec(x.shape, lambda i: (0, 0, 0)), pl.BlockSpec(mod.shape, lambda i: (0, 0, 0)),
                  pl.BlockSpec((None, d_model, n_out), lambda i: (layer, 0, 0))],
        out_specs=[row_spec(POOL_WIDTH), row_spec(sb), row_spec(sb), row_spec(sb), row_spec(sb), row_spec(sb)],
        out_shape=[jax.ShapeDtypeStruct((rows, POOL_WIDTH), F32),
                   jax.ShapeDtypeStruct((rows, sb), F32),
                   jax.ShapeDtypeStruct((rows, sb), F32),
                   jax.ShapeDtypeStruct((rows, sb), BF16),
                   jax.ShapeDtypeStruct((rows, sb), BF16),
                   jax.ShapeDtypeStruct((rows, sb), BF16)],
        compiler_params=pltpu.CompilerParams(
            dimension_semantics=("arbitrary",), vmem_limit_bytes=V7X_VMEM_LIMIT_BYTES),
        name="inproj",
    )(x, mod, w_in_bf)


def _inproj_prompt_kernel(x_ref, mod_ref, w_uq_ref, w_kv_t_ref, *refs, d_model, layer, key_block):
    if layer:
        prev_k_ref, prev_v_ref, *refs = refs
    else:
        prev_k_ref = prev_v_ref = None
    u_ref, k_all_ref, v_all_ref, qb_ref, kb_ref, vb_ref = refs
    mod = mod_ref[...]
    shift = mod[..., 0:d_model]
    scale = mod[..., d_model:2 * d_model]
    h = (_layer_norm(x_ref[...]) * (1.0 + scale) + shift).astype(BF16)
    uq = _dot(h, w_uq_ref[...])
    kv_t = _dot_nt(w_kv_t_ref[...], h)
    sb = kv_t.shape[0] // 2
    u_ref[...] = uq[:, :POOL_WIDTH]
    qb_ref[...] = (uq[:, POOL_WIDTH:] * (SB_SCALE * LOG2_E)).astype(BF16)
    n_blocks = kv_t.shape[1] // key_block
    for all_ref, bf_ref, prev_ref, val in ((k_all_ref, kb_ref, prev_k_ref, kv_t[:sb]),
                                           (v_all_ref, vb_ref, prev_v_ref, kv_t[sb:])):
        if layer:
            all_ref[0:layer] = prev_ref[...]
        all_ref[layer] = val
        for c in range(n_blocks):
            bf_ref[c] = val[:, c * key_block:(c + 1) * key_block].astype(BF16)


def _inproj_prompt(x, mod, w_uq_bf, w_kv_t_bf, layer, prev_k, prev_v):
    seq, d_model = x.shape
    n_uq = w_uq_bf.shape[-1]
    sb = w_kv_t_bf.shape[1] // 2
    tm = ROW_TILE
    key_block = ATTN_TILE
    row_spec = lambda w: pl.BlockSpec((tm, w), lambda i: (i, 0))
    stacked_spec = lambda n: pl.BlockSpec((n, sb, tm), lambda i: (0, 0, i))
    blocks_spec = pl.BlockSpec((tm // key_block, sb, key_block), lambda i: (i, 0, 0))
    in_specs = [row_spec(d_model), pl.BlockSpec(mod.shape, lambda i: (0, 0)),
                pl.BlockSpec((None, d_model, n_uq), lambda i: (layer, 0, 0)),
                pl.BlockSpec((None, 2 * sb, d_model), lambda i: (layer, 0, 0))]
    operands = [x, mod, w_uq_bf, w_kv_t_bf]
    if layer:
        in_specs += [stacked_spec(layer), stacked_spec(layer)]
        operands += [prev_k, prev_v]
    return pl.pallas_call(
        functools.partial(_inproj_prompt_kernel, d_model=d_model, layer=layer, key_block=key_block),
        grid=(seq // tm,),
        in_specs=in_specs,
        out_specs=[row_spec(POOL_WIDTH), stacked_spec(layer + 1), stacked_spec(layer + 1), row_spec(sb),
                   blocks_spec, blocks_spec],
        out_shape=[jax.ShapeDtypeStruct((seq, POOL_WIDTH), F32),
                   jax.ShapeDtypeStruct((layer + 1, sb, seq), F32),
                   jax.ShapeDtypeStruct((layer + 1, sb, seq), F32),
                   jax.ShapeDtypeStruct((seq, sb), BF16),
                   jax.ShapeDtypeStruct((seq // key_block, sb, key_block), BF16),
                   jax.ShapeDtypeStruct((seq // key_block, sb, key_block), BF16)],
        compiler_params=pltpu.CompilerParams(
            dimension_semantics=("arbitrary",), vmem_limit_bytes=V7X_VMEM_LIMIT_BYTES),
        name="inproj_prompt",
    )(*operands)


def _sb_bits(z2s, suffix, mask):
    bits = [_softplus2(z2) for z2 in z2s]
    if mask is not None:
        bits = [jnp.where(mask, b, 0.0) for b in bits]
    parts = [_split_bf16(b) for b in bits]
    later = [_dot(hi, suffix) + _dot(lo, suffix) for hi, lo in parts]
    return later, [jnp.sum(b, axis=-1, keepdims=True) for b in bits]


def _sb_weighted(z2s, later, carries, v_bfs, mask, v_transposed=False):
    ps = [jnp.exp2(z2 - (l + c)) for z2, l, c in zip(z2s, later, carries)]
    if mask is not None:
        ps = [jnp.where(mask, p, 0.0) for p in ps]
    pv_dot = _dot_nt if v_transposed else _dot
    return [pv_dot(p.astype(BF16), v) for p, v in zip(ps, v_bfs)]


def _sb_blocks(z2s, v_bfs, suffix, carries, mask, v_transposed=False):
    later, sums = _sb_bits(z2s, suffix, mask)
    return _sb_weighted(z2s, later, carries, v_bfs, mask, v_transposed), sums


def _sb_chain(z2s, v_bfs, suffix, carry, mask, v_transposed=False):
    later, sums = _sb_bits(z2s, suffix, mask)
    carries = [carry]
    for block_sum in sums:
        carries.append(carries[-1] + block_sum)
    pvs = _sb_weighted(z2s, later, carries[:-1], v_bfs, mask, v_transposed)
    total = pvs[0]
    for pv in pvs[1:]:
        total = total + pv
    return total, carries[-1]


def _attn_prompt_kernel(q_ref, k_ref, v_ref, o_ref, carry_ref, acc_ref, *, tile, heads):
    i = pl.program_id(0)
    lane = lax.broadcasted_iota(jnp.int32, (tile, HEAD_PAIR_WIDTH), 1)
    low = lane < SB_HEAD_DIM
    q_heads = []
    for h in range(heads):
        qp = q_ref[:, (h // 2) * HEAD_PAIR_WIDTH:(h // 2 + 1) * HEAD_PAIR_WIDTH]
        q_heads.append(jnp.where(low if h % 2 == 0 else ~low, qp, jnp.zeros_like(qp)))
    suffix = _suffix_ones(tile)
    row = lax.broadcasted_iota(jnp.int32, (tile, tile), 0)
    col = lax.broadcasted_iota(jnp.int32, (tile, tile), 1)
    causal = col < row

    def block(j, mask, first):
        pair_rows = lambda h: slice((h // 2) * HEAD_PAIR_WIDTH, (h // 2 + 1) * HEAD_PAIR_WIDTH)
        z2s = [_dot(q_heads[h], k_ref[j, pair_rows(h), :]) for h in range(heads)]
        v_blks = [v_ref[j, pair_rows(h), :] for h in range(heads)]
        carries = [jnp.zeros((tile, 1), F32) if first else carry_ref[h] for h in range(heads)]
        pvs, blk_bits = _sb_blocks(z2s, v_blks, suffix, carries, mask, v_transposed=True)
        least = None
        for h in range(heads):
            carry = carries[h] + blk_bits[h]
            carry_ref[h] = carry
            acc_ref[h] = pvs[h] if first else acc_ref[h] + pvs[h]
            least = carry if least is None else jnp.minimum(least, carry)
        return jnp.min(least)

    least = block(i, causal, True)

    def more(state):
        j, least = state
        return (j >= 0) & (least < F32_UNDERFLOW_BITS)

    lax.while_loop(more, lambda state: (state[0] - 1, block(state[0], None, False)), (i - 1, least))
    for p in range(heads // 2):
        o_ref[:, p * HEAD_PAIR_WIDTH:(p + 1) * HEAD_PAIR_WIDTH] = jnp.where(low, acc_ref[2 * p], acc_ref[2 * p + 1])


def _attn_prompt(q_bf, k_bf, v_bf):
    seq, width = q_bf.shape
    tile = ATTN_TILE
    heads = width // SB_HEAD_DIM
    resident = lambda: pl.BlockSpec(k_bf.shape, lambda i: (0, 0, 0), pipeline_mode=pl.Buffered(1))
    return pl.pallas_call(
        functools.partial(_attn_prompt_kernel, tile=tile, heads=heads),
        grid=(seq // tile,),
        in_specs=[pl.BlockSpec((tile, width), lambda i: (i, 0)), resident(), resident()],
        out_specs=pl.BlockSpec((tile, width), lambda i: (i, 0)),
        out_shape=jax.ShapeDtypeStruct((seq, width), F32),
        scratch_shapes=[pltpu.VMEM((heads, tile, 1), F32),
                        pltpu.VMEM((heads, tile, HEAD_PAIR_WIDTH), F32)],
        compiler_params=pltpu.CompilerParams(
            dimension_semantics=("arbitrary",), vmem_limit_bytes=V7X_VMEM_LIMIT_BYTES),
        name="attn_prompt",
    )(q_bf, k_bf, v_bf)


def _attn_sample_kernel(q_ref, kn_ref, vn_ref, ck_hbm, cv_hbm, o_ref, kbuf, vbuf, sem, qs_ref, carry_ref, acc_ref,
                        *, layer, heads, past):
    n_batch, t_new, width = q_ref.shape
    q_rows = heads * t_new
    chunk = SAMPLE_CHUNK_TOKENS
    n_chunks = past // chunk
    lane = lax.broadcasted_iota(jnp.int32, (t_new, width), 1)
    in_head = [(lane >= h * SB_HEAD_DIM) & (lane < (h + 1) * SB_HEAD_DIM) for h in range(heads)]
    row = lax.broadcasted_iota(jnp.int32, (q_rows, t_new), 0)
    col = lax.broadcasted_iota(jnp.int32, (q_rows, t_new), 1)
    causal_new = col < (row % t_new)
    suffix_new = _suffix_ones(t_new)
    suffix = _suffix_ones(chunk)

    def chunk_copies(b, c, slot):
        tokens = pl.ds(pl.multiple_of(past - (c + 1) * chunk, chunk), chunk)
        return (pltpu.make_async_copy(ck_hbm.at[layer, b, :, :, tokens], kbuf.at[slot], sem.at[0, slot]),
                pltpu.make_async_copy(cv_hbm.at[layer, b, :, :, tokens], vbuf.at[slot], sem.at[1, slot]))

    def add_chunk(slot):
        k_t = kbuf[slot].reshape(width, chunk).astype(BF16)
        v_t = vbuf[slot].reshape(width, chunk).astype(BF16)
        pv, carry = _sb_chain([_dot(qs_ref[...], k_t)], [v_t], suffix, carry_ref[...], None, v_transposed=True)
        acc_ref[...] += pv
        carry_ref[...] = carry
        return jnp.min(carry)

    def one_batch(b, _):
        @pl.when(b + 1 < n_batch)
        def _():
            for cp in chunk_copies(b + 1, 0, (b + 1) % 2):
                cp.start()

        q = q_ref[b]
        for h in range(heads):
            qs_ref[h * t_new:(h + 1) * t_new, :] = jnp.where(in_head[h], q, jnp.zeros_like(q))
        pv, carry = _sb_chain([_dot_nt(qs_ref[...], kn_ref[b])], [vn_ref[b]], suffix_new,
                              jnp.zeros((q_rows, 1), F32), causal_new)
        acc_ref[...] = pv
        carry_ref[...] = carry
        for cp in chunk_copies(b, 0, b % 2):
            cp.wait()
        least = add_chunk(b % 2)

        def more(state):
            c, least = state
            return (c < n_chunks) & (least < F32_UNDERFLOW_BITS)

        def older_chunk(state):
            c, _ = state
            copies = chunk_copies(b, c, 2)
            for cp in copies:
                cp.start()
            for cp in copies:
                cp.wait()
            return c + 1, add_chunk(2)

        lax.while_loop(more, older_chunk, (1, least))
        out = jnp.zeros((t_new, width), F32)
        for h in range(heads):
            out = jnp.where(in_head[h], acc_ref[h * t_new:(h + 1) * t_new, :], out)
        o_ref[b] = out
        return 0

    for cp in chunk_copies(0, 0, 0):
        cp.start()
    lax.fori_loop(0, n_batch, one_batch, 0)


def _attn_sample(q_bf, k_bf, v_bf, cache_k_t, cache_v_t, layer):
    n_batch, t_new, width = q_bf.shape
    heads, head_dim, past = cache_k_t.shape[2:]
    q_rows = heads * t_new
    full = lambda a: pl.BlockSpec(a.shape, lambda i: (0,) * a.ndim)
    chunk_buffers = pltpu.VMEM((3, heads, head_dim, SAMPLE_CHUNK_TOKENS), F32)
    return pl.pallas_call(
        functools.partial(_attn_sample_kernel, layer=layer, heads=heads, past=past),
        grid=(1,),
        in_specs=[full(q_bf), full(k_bf), full(v_bf),
                  pl.BlockSpec(memory_space=pl.ANY), pl.BlockSpec(memory_space=pl.ANY)],
        out_specs=full(q_bf),
        out_shape=jax.ShapeDtypeStruct(q_bf.shape, F32),
        scratch_shapes=[chunk_buffers, chunk_buffers, pltpu.SemaphoreType.DMA((2, 3)),
                        pltpu.VMEM((q_rows, width), BF16),
                        pltpu.VMEM((q_rows, 1), F32),
                        pltpu.VMEM((q_rows, width), F32)],
        compiler_params=pltpu.CompilerParams(
            dimension_semantics=("arbitrary",), vmem_limit_bytes=V7X_VMEM_LIMIT_BYTES),
        name="attn_sample",
    )(q_bf, k_bf, v_bf, cache_k_t, cache_v_t)


def _trailing_window_sums(u_ext, axis):
    n = u_ext.shape[axis]
    t = n - HALO_ROWS
    sl = lambda a, lo, hi: lax.slice_in_dim(a, lo, hi, axis=axis)
    cols = lambda a, g: a[..., g * POOL_GROUP_WIDTH:]
    s2 = sl(u_ext, 1, n) + sl(u_ext, 0, n - 1)
    s2b = cols(s2, 1)
    s4 = sl(s2b, 2, n - 1) + sl(s2b, 0, n - 3)
    s4b = s4[..., POOL_GROUP_WIDTH:]
    s8 = sl(s4b, 4, n - 3) + sl(s4b, 0, n - 7)
    s8b = s8[..., POOL_GROUP_WIDTH:]
    s16 = sl(s8b, 8, n - 7) + sl(s8b, 0, n - 15)
    first = HALO_ROWS
    return (sl(s2, first - 1, first - 1 + t)[..., :POOL_GROUP_WIDTH],
            sl(s4, first - 3, first - 3 + t)[..., :POOL_GROUP_WIDTH],
            sl(s8, first - 7, first - 7 + t)[..., :POOL_GROUP_WIDTH],
            sl(s16, first - 15, first - 15 + t))


def _post_kernel(x_ref, u_ref, halo_ref, s_ref, mod_ref, wpool_ref, pscale_ref, wo_ref, g1_ref, b1_ref,
                 wgu_ref, wdown_ref, g2_ref, b2_ref, o_ref, *, d_model, d_ff, alpha, ff_chunks, has_state, n_parts):
    mod = mod_ref[...]
    gate1 = mod[..., 2 * d_model:3 * d_model]
    shift2 = mod[..., 3 * d_model:4 * d_model]
    scale2 = mod[..., 4 * d_model:5 * d_model]
    gate2 = mod[..., 5 * d_model:6 * d_model]
    row_axis = x_ref.ndim - 2
    rows = x_ref.shape[row_axis] // n_parts
    assert has_state <= (n_parts == 1) and rows >= HALO_ROWS
    parts = [slice(p * rows, (p + 1) * rows) for p in range(n_parts)]
    in_part = lambda ref, part: ref[...] if has_state else ref[part, :]

    def pooled_mix_inputs(p):
        u = in_part(u_ref, parts[p])
        if has_state:
            halo = halo_ref[...]
            inv_counts = [1.0 / w for w in POOL_WINDOWS]
        else:
            if p == 0:
                halo = halo_ref[...]
                halo = jnp.where(pl.program_id(0) == 0, jnp.zeros_like(halo), halo)
            else:
                halo = u_ref[p * rows - HALO_ROWS:p * rows, :]
            t = (pl.program_id(0) * n_parts + p) * rows + lax.broadcasted_iota(jnp.int32, (rows, 1), 0)
            avail = (t + 1).astype(F32)
            inv_counts = [1.0 / jnp.minimum(avail, float(w)) for w in POOL_WINDOWS]
        sums = _trailing_window_sums(jnp.concatenate([halo, u], axis=row_axis), row_axis)
        pool_out = []
        for g, (win_sum, inv) in enumerate(zip(sums, inv_counts)):
            cur = u[..., g * POOL_GROUP_WIDTH:(g + 1) * POOL_GROUP_WIDTH]
            pooled = (win_sum * inv - cur).reshape(-1, POOL_GROUP_WIDTH)
            pool_out.append(_dot(pooled.astype(BF16), wpool_ref[g]))
        pool_out = jnp.concatenate(pool_out, axis=-1) * pscale_ref[...]
        sb_out = in_part(s_ref, parts[p])
        return pool_out.astype(BF16), sb_out.reshape(-1, sb_out.shape[-1]).astype(BF16)

    every = range(n_parts)
    xs = [in_part(x_ref, parts[p]) for p in every]
    mix_in = [pooled_mix_inputs(p) for p in every]
    mix = [_dot(pool_bf, wo_ref[0:POOL_WIDTH, :]) + _dot(sb_bf, wo_ref[POOL_WIDTH:, :]) for pool_bf, sb_bf in mix_in]
    x1 = [_layer_norm(alpha * xs[p] + gate1 * mix[p].reshape(xs[p].shape)) * g1_ref[...] + b1_ref[...] for p in every]
    h = [(_layer_norm(x1[p]) * (1.0 + scale2) + shift2).astype(BF16).reshape(-1, d_model) for p in every]
    ff = [None] * n_parts
    for lo, hi in ff_chunks:
        gate = [_dot(h[p], wgu_ref[:, lo:hi]) for p in every]
        up = [_dot(h[p], wgu_ref[:, d_ff + lo:d_ff + hi]) for p in every]
        act = [(_silu(gate[p]) * up[p]).astype(BF16) for p in every]
        part = [_dot(act[p], wdown_ref[lo:hi, :]) for p in every]
        ff = [part[p] if ff[p] is None else ff[p] + part[p] for p in every]
    for p in every:
        out = _layer_norm(alpha * x1[p] + gate2 * ff[p].reshape(xs[p].shape)) * g2_ref[...] + b2_ref[...]
        if has_state:
            o_ref[...] = out
        else:
            o_ref[parts[p], :] = out


def _ff_chunks(d_ff):
    mxu_cols = 256
    tiles = d_ff // mxu_cols
    assert tiles * mxu_cols == d_ff
    split = (tiles + 1) // 2 * mxu_cols
    return ((0, split), (split, d_ff))


def _post(x, u, halo_src, s, mod, layer, w_pool_bf, pool_scale, w_o_bf, ln1_g, ln1_b, w_gu_bf, w_down_bf,
          ln2_g, ln2_b, alpha):
    d_model = x.shape[-1]
    d_ff = w_down_bf.shape[1]
    depth = w_o_bf.shape[0]
    has_state = x.ndim == 3
    if has_state:
        grid = (1,)
        full3 = lambda a: pl.BlockSpec(a.shape, lambda i: (0, 0, 0))
        x_spec, u_spec, halo_spec, s_spec, mod_spec, out_spec = full3(x), full3(u), full3(halo_src), full3(s), full3(mod), full3(x)
    else:
        rows = x.shape[0]
        tm = ROW_TILE
        grid = (rows // tm,)
        halo_blocks = tm // HALO_ROWS
        x_spec = pl.BlockSpec((tm, d_model), lambda i: (i, 0))
        u_spec = pl.BlockSpec((tm, POOL_WIDTH), lambda i: (i, 0))
        halo_spec = pl.BlockSpec((HALO_ROWS, POOL_WIDTH), lambda i: (jnp.maximum(i * halo_blocks - 1, 0), 0))
        s_spec = pl.BlockSpec((tm, s.shape[-1]), lambda i: (i, 0))
        mod_spec = pl.BlockSpec(mod.shape, lambda i: (0, 0))
        out_spec = x_spec
    const = pl.Buffered(1)

    def layer_spec(a):
        nd = a.ndim - 1
        return pl.BlockSpec((None,) + a.shape[1:], lambda i: (layer,) + (0,) * nd, pipeline_mode=const)

    vec = lambda a: a.reshape(depth, 1, a.shape[-1])
    return pl.pallas_call(
        functools.partial(_post_kernel, d_model=d_model, d_ff=d_ff, alpha=alpha, ff_chunks=_ff_chunks(d_ff),
                          has_state=has_state, n_parts=1 if has_state else POST_ROW_PARTS),
        grid=grid,
        in_specs=[x_spec, u_spec, halo_spec, s_spec, mod_spec,
                  layer_spec(w_pool_bf), layer_spec(vec(pool_scale)), layer_spec(w_o_bf),
                  layer_spec(vec(ln1_g)), layer_spec(vec(ln1_b)),
                  layer_spec(w_gu_bf), layer_spec(w_down_bf),
                  layer_spec(vec(ln2_g)), layer_spec(vec(ln2_b))],
        out_specs=out_spec,
        out_shape=jax.ShapeDtypeStruct(x.shape, F32),
        compiler_params=pltpu.CompilerParams(
            dimension_semantics=("arbitrary",), vmem_limit_bytes=V7X_VMEM_LIMIT_BYTES),
        name="post",
    )(x, u, halo_src, s, mod, w_pool_bf, vec(pool_scale), w_o_bf, vec(ln1_g), vec(ln1_b),
      w_gu_bf, w_down_bf, vec(ln2_g), vec(ln2_b))


def kernel(x_prompt, x_sample, cache_k, cache_v, state_pool, c_prompt, c_sample, w_ada, b_ada, w_in, w_pool,
           pool_scale, w_o, ln1_g, ln1_b, w_gu, w_down, ln2_g, ln2_b):
    depth, d_model, _ = w_ada.shape
    batch, seq, _ = x_prompt.shape
    dec_batch, dec_seq, _ = x_sample.shape
    past = cache_k.shape[2]
    heads, head_dim = cache_k.shape[3], cache_k.shape[4]
    assert batch == 1 and head_dim == SB_HEAD_DIM and state_pool.shape[2] == POOL_HIST
    assert seq % ROW_TILE == 0 and past % SAMPLE_CHUNK_TOKENS == 0 and dec_seq >= POOL_HIST
    alpha = (2 * depth) ** 0.25

    n_cond = batch + dec_batch
    c_all = jnp.concatenate([c_prompt, c_sample], axis=0)
    c_all = jnp.pad(c_all, ((0, -n_cond % 16), (0, 0)))
    mod = _adaln(c_all, w_ada, b_ada)

    w_in_bf, w_pool_bf, w_o_bf = w_in.astype(BF16), w_pool.astype(BF16), w_o.astype(BF16)
    n_uq = w_in.shape[-1] - 2 * heads * head_dim
    w_uq_bf = w_in_bf[:, :, :n_uq]
    w_kv_t_bf = w_in_bf[:, :, n_uq:].transpose(0, 2, 1)
    w_gu_bf, w_down_bf = w_gu.astype(BF16), w_down.astype(BF16)
    cache_k_t = cache_k.transpose(0, 1, 3, 4, 2)
    cache_v_t = cache_v.transpose(0, 1, 3, 4, 2)
    hist = jnp.pad(state_pool, ((0, 0), (0, 0), (HALO_ROWS - POOL_HIST, 0), (0, 0)))

    xp = x_prompt.reshape(seq, d_model)
    xs = x_sample
    k_p = v_p = None
    pool_p, k_s, v_s, pool_s = [], [], [], []
    for l in range(depth):
        post_w = (l, w_pool_bf, pool_scale, w_o_bf, ln1_g, ln1_b, w_gu_bf, w_down_bf, ln2_g, ln2_b, alpha)

        mod_p = mod[l, 0:batch]
        u, k_p, v_p, qb, kb, vb = _inproj_prompt(xp, mod_p, w_uq_bf, w_kv_t_bf, l, k_p, v_p)
        s = _attn_prompt(qb, kb, vb)
        xp = _post(xp, u, u, s, mod_p, *post_w)
        pool_p.append(u[seq - POOL_HIST:].reshape(batch, POOL_HIST, POOL_WIDTH))

        mod_s = mod[l, batch:n_cond].reshape(dec_batch, 1, -1)
        u, k, v, qb, kb, vb = _inproj(xs, mod_s, w_in_bf, l)
        to3 = lambda a: a.reshape(dec_batch, dec_seq, a.shape[-1])
        s = _attn_sample(to3(qb), to3(kb), to3(vb), cache_k_t, cache_v_t, l)
        u3 = to3(u)
        xs = _post(xs, u3, hist[l], s, mod_s, *post_w)
        k_s.append(k.reshape(dec_batch, dec_seq, heads, head_dim))
        v_s.append(v.reshape(dec_batch, dec_seq, heads, head_dim))
        pool_s.append(u3[:, dec_seq - POOL_HIST:])

    by_token = lambda a: a.reshape(depth, batch, heads, head_dim, seq).transpose(0, 1, 4, 2, 3)
    return (xp.reshape(batch, seq, d_model), xs, by_token(k_p), by_token(v_p), jnp.stack(pool_p),
            jnp.stack(k_s), jnp.stack(v_s), jnp.stack(pool_s))
```

```python
import functools
import math

import jax
import jax.numpy as jnp
from jax import lax
from jax.experimental import pallas as pl
from jax.experimental.pallas import tpu as pltpu

F32 = jnp.float32
BF16 = jnp.bfloat16

POOL_WINDOWS = (2, 4, 8, 16)
POOL_GROUP_WIDTH = 128
POOL_WIDTH = POOL_GROUP_WIDTH * len(POOL_WINDOWS)
POOL_HIST = max(POOL_WINDOWS) - 1
HALO_ROWS = 16
SB_HEAD_DIM = 64
SB_SCALE = 1.0 / math.sqrt(SB_HEAD_DIM)
LOG2_E = 1.0 / math.log(2.0)
HEAD_PAIR_WIDTH = 2 * SB_HEAD_DIM
LN_EPS = 1e-5
F32_UNDERFLOW_BITS = 151.0

V7X_VMEM_LIMIT_BYTES = 56 * 1024 * 1024

ROW_TILE = 512
POST_ROW_PARTS = 2
ATTN_TILE = 256
SAMPLE_CHUNK_TOKENS = 256
ADA_COL_TILE = 1536


def _dot(a, b):
    return jnp.dot(a, b, preferred_element_type=F32)


def _dot_nt(a, b):
    return lax.dot_general(a, b, (((1,), (1,)), ((), ())), preferred_element_type=F32)


def _split_bf16(x):
    hi = x.astype(BF16)
    lo = (x - hi.astype(F32)).astype(BF16)
    return hi, lo


def _layer_norm(x):
    mu = jnp.mean(x, axis=-1, keepdims=True)
    xc = x - mu
    var = jnp.mean(xc * xc, axis=-1, keepdims=True)
    return xc * lax.rsqrt(var + LN_EPS)


def _silu(x):
    return x / (1.0 + jnp.exp(-x))


def _softplus2(z2):
    return jnp.maximum(z2, 0.0) + LOG2_E * jnp.log(1.0 + jnp.exp2(-jnp.abs(z2)))


def _suffix_ones(n):
    r = lax.broadcasted_iota(jnp.int32, (n, n), 0)
    c = lax.broadcasted_iota(jnp.int32, (n, n), 1)
    return jnp.where(r >= c, 1.0, 0.0).astype(BF16)


def _adaln_kernel(c_ref, w_ref, b_ref, o_ref):
    cond = _silu(c_ref[...])
    c_hi, c_lo = _split_bf16(cond)
    w_hi, w_lo = _split_bf16(w_ref[...])
    o_ref[...] = _dot(c_hi, w_hi) + _dot(c_hi, w_lo) + _dot(c_lo, w_hi) + b_ref[...]


def _adaln(c_all, w_ada, b_ada):
    depth, d_model, n_out = w_ada.shape
    rows = c_all.shape[0]
    tn = ADA_COL_TILE
    return pl.pallas_call(
        _adaln_kernel,
        grid=(depth, n_out // tn),
        in_specs=[
            pl.BlockSpec((rows, d_model), lambda l, j: (0, 0)),
            pl.BlockSpec((None, d_model, tn), lambda l, j: (l, 0, j)),
            pl.BlockSpec((None, 1, tn), lambda l, j: (l, 0, j)),
        ],
        out_specs=pl.BlockSpec((None, rows, tn), lambda l, j: (l, 0, j)),
        out_shape=jax.ShapeDtypeStruct((depth, rows, n_out), F32),
        compiler_params=pltpu.CompilerParams(
            dimension_semantics=("arbitrary", "arbitrary"), vmem_limit_bytes=V7X_VMEM_LIMIT_BYTES),
        name="adaln",
    )(c_all, w_ada, b_ada.reshape(depth, 1, n_out))


def _inproj_kernel(x_ref, mod_ref, w_ref, u_ref, k_ref, v_ref, qb_ref, kb_ref, vb_ref, *, d_model):
    x = x_ref[...]
    mod = mod_ref[...]
    shift = mod[..., 0:d_model]
    scale = mod[..., d_model:2 * d_model]
    h = (_layer_norm(x) * (1.0 + scale) + shift).astype(BF16)
    h = h.reshape(-1, d_model)
    proj = _dot(h, w_ref[...])
    sb = (proj.shape[-1] - POOL_WIDTH) // 3
    o = POOL_WIDTH
    u = proj[:, :o]
    q = proj[:, o:o + sb]
    k = proj[:, o + sb:o + 2 * sb]
    v = proj[:, o + 2 * sb:o + 3 * sb]
    u_ref[...] = u
    k_ref[...] = k
    v_ref[...] = v
    qb_ref[...] = (q * (SB_SCALE * LOG2_E)).astype(BF16)
    kb_ref[...] = k.astype(BF16)
    vb_ref[...] = v.astype(BF16)


def _inproj(x, mod, w_in_bf, layer):
    d_model = x.shape[-1]
    n_out = w_in_bf.shape[-1]
    sb = (n_out - POOL_WIDTH) // 3
    rows = x.shape[0] * x.shape[1]
    row_spec = lambda w: pl.BlockSpec((rows, w), lambda i: (0, 0))
    return pl.pallas_call(
        functools.partial(_inproj_kernel, d_model=d_model),
        grid=(1,),
        in_specs=[pl.BlockSpec(x.shape, lambda i: (0, 0, 0)), pl.BlockSpec(mod.shape, lambda i: (0, 0, 0)),
                  pl.BlockSpec((None, d_model, n_out), lambda i: (layer, 0, 0))],
        out_specs=[row_spec(POOL_WIDTH), row_spec(sb), row_spec(sb), row_spec(sb), row_spec(sb), row_spec(sb)],
        out_shape=[jax.ShapeDtypeStruct((rows, POOL_WIDTH), F32),
                   jax.ShapeDtypeStruct((rows, sb), F32),
                   jax.ShapeDtypeStruct((rows, sb), F32),
                   jax.ShapeDtypeStruct((rows, sb), BF16),
                   jax.ShapeDtypeStruct((rows, sb), BF16),
                   jax.ShapeDtypeStruct((rows, sb), BF16)],
        compiler_params=pltpu.CompilerParams(
            dimension_semantics=("arbitrary",), vmem_limit_bytes=V7X_VMEM_LIMIT_BYTES),
        name="inproj",
    )(x, mod, w_in_bf)


def _inproj_prompt_kernel(x_ref, mod_ref, w_uq_ref, w_kv_t_ref, u_ref, k_ref, v_ref, qb_ref, kb_ref, vb_ref,
                          *, d_model, key_block):
    mod = mod_ref[...]
    shift = mod[..., 0:d_model]
    scale = mod[..., d_model:2 * d_model]
    h = (_layer_norm(x_ref[...]) * (1.0 + scale) + shift).astype(BF16)
    uq = _dot(h, w_uq_ref[...])
    kv_t = _dot_nt(w_kv_t_ref[...], h)
    sb = kv_t.shape[0] // 2
    u_ref[...] = uq[:, :POOL_WIDTH]
    qb_ref[...] = (uq[:, POOL_WIDTH:] * (SB_SCALE * LOG2_E)).astype(BF16)
    n_blocks = kv_t.shape[1] // key_block
    for f32_ref, bf_ref, val in ((k_ref, kb_ref, kv_t[:sb]), (v_ref, vb_ref, kv_t[sb:])):
        f32_ref[...] = val
        for c in range(n_blocks):
            bf_ref[c] = val[:, c * key_block:(c + 1) * key_block].astype(BF16)


def _inproj_prompt(x, mod, w_uq_bf, w_kv_t_bf, layer):
    seq, d_model = x.shape
    n_uq = w_uq_bf.shape[-1]
    sb = w_kv_t_bf.shape[1] // 2
    tm = ROW_TILE
    key_block = ATTN_TILE
    row_spec = lambda w: pl.BlockSpec((tm, w), lambda i: (i, 0))
    t_spec = pl.BlockSpec((sb, tm), lambda i: (0, i))
    blocks_spec = pl.BlockSpec((tm // key_block, sb, key_block), lambda i: (i, 0, 0))
    return pl.pallas_call(
        functools.partial(_inproj_prompt_kernel, d_model=d_model, key_block=key_block),
        grid=(seq // tm,),
        in_specs=[row_spec(d_model), pl.BlockSpec(mod.shape, lambda i: (0, 0)),
                  pl.BlockSpec((None, d_model, n_uq), lambda i: (layer, 0, 0)),
                  pl.BlockSpec((None, 2 * sb, d_model), lambda i: (layer, 0, 0))],
        out_specs=[row_spec(POOL_WIDTH), t_spec, t_spec, row_spec(sb), blocks_spec, blocks_spec],
        out_shape=[jax.ShapeDtypeStruct((seq, POOL_WIDTH), F32),
                   jax.ShapeDtypeStruct((sb, seq), F32),
                   jax.ShapeDtypeStruct((sb, seq), F32),
                   jax.ShapeDtypeStruct((seq, sb), BF16),
                   jax.ShapeDtypeStruct((seq // key_block, sb, key_block), BF16),
                   jax.ShapeDtypeStruct((seq // key_block, sb, key_block), BF16)],
        compiler_params=pltpu.CompilerParams(
            dimension_semantics=("arbitrary",), vmem_limit_bytes=V7X_VMEM_LIMIT_BYTES),
        name="inproj_prompt",
    )(x, mod, w_uq_bf, w_kv_t_bf)


def _sb_bits(z2s, suffix, mask):
    later, sums = [], []
    for z2 in z2s:
        bits = _softplus2(z2)
        if mask is not None:
            bits = jnp.where(mask, bits, 0.0)
        later.append(_dot(bits.astype(BF16), suffix))
        sums.append(jnp.sum(bits, axis=-1, keepdims=True))
    return later, sums


def _sb_weighted(z2s, later, carries, v_bfs, mask, v_transposed=False):
    pv_dot = _dot_nt if v_transposed else _dot
    pvs = []
    for z2, l, c, v in zip(z2s, later, carries, v_bfs):
        p = jnp.exp2(z2 - (l + c))
        if mask is not None:
            p = jnp.where(mask, p, 0.0)
        pvs.append(pv_dot(p.astype(BF16), v))
    return pvs


def _sb_blocks(z2s, v_bfs, suffix, carries, mask, v_transposed=False):
    later, sums = _sb_bits(z2s, suffix, mask)
    return _sb_weighted(z2s, later, carries, v_bfs, mask, v_transposed), sums


def _sb_chain(z2s, v_bfs, suffix, carry, mask, v_transposed=False):
    later, sums = _sb_bits(z2s, suffix, mask)
    carries = [carry]
    for block_sum in sums:
        carries.append(carries[-1] + block_sum)
    pvs = _sb_weighted(z2s, later, carries[:-1], v_bfs, mask, v_transposed)
    total = pvs[0]
    for pv in pvs[1:]:
        total = total + pv
    return total, carries[-1]


def _attn_prompt_kernel(q_ref, k_ref, v_ref, o_ref, carry_ref, acc_ref, *, tile, heads):
    i = pl.program_id(0)
    lane = lax.broadcasted_iota(jnp.int32, (tile, HEAD_PAIR_WIDTH), 1)
    low = lane < SB_HEAD_DIM
    q_heads = []
    for h in range(heads):
        qp = q_ref[:, (h // 2) * HEAD_PAIR_WIDTH:(h // 2 + 1) * HEAD_PAIR_WIDTH]
        q_heads.append(jnp.where(low if h % 2 == 0 else ~low, qp, jnp.zeros_like(qp)))
    suffix = _suffix_ones(tile)
    row = lax.broadcasted_iota(jnp.int32, (tile, tile), 0)
    col = lax.broadcasted_iota(jnp.int32, (tile, tile), 1)
    causal = col < row

    def block(j, mask, first):
        pair_rows = lambda h: slice((h // 2) * HEAD_PAIR_WIDTH, (h // 2 + 1) * HEAD_PAIR_WIDTH)
        z2s = [_dot(q_heads[h], k_ref[j, pair_rows(h), :]) for h in range(heads)]
        v_blks = [v_ref[j, pair_rows(h), :] for h in range(heads)]
        carries = [jnp.zeros((tile, 1), F32) if first else carry_ref[h] for h in range(heads)]
        pvs, blk_bits = _sb_blocks(z2s, v_blks, suffix, carries, mask, v_transposed=True)
        least = None
        for h in range(heads):
            carry = carries[h] + blk_bits[h]
            carry_ref[h] = carry
            acc_ref[h] = pvs[h] if first else acc_ref[h] + pvs[h]
            least = carry if least is None else jnp.minimum(least, carry)
        return jnp.min(least)

    least = block(i, causal, True)

    def more(state):
        j, least = state
        return (j >= 0) & (least < F32_UNDERFLOW_BITS)

    lax.while_loop(more, lambda state: (state[0] - 1, block(state[0], None, False)), (i - 1, least))
    for p in range(heads // 2):
        out = jnp.where(low, acc_ref[2 * p], acc_ref[2 * p + 1])
        o_ref[:, p * HEAD_PAIR_WIDTH:(p + 1) * HEAD_PAIR_WIDTH] = out.astype(o_ref.dtype)


def _attn_prompt(q_bf, k_bf, v_bf):
    seq, width = q_bf.shape
    tile = ATTN_TILE
    heads = width // SB_HEAD_DIM
    resident = lambda: pl.BlockSpec(k_bf.shape, lambda i: (0, 0, 0), pipeline_mode=pl.Buffered(1))
    return pl.pallas_call(
        functools.partial(_attn_prompt_kernel, tile=tile, heads=heads),
        grid=(seq // tile,),
        in_specs=[pl.BlockSpec((tile, width), lambda i: (i, 0)), resident(), resident()],
        out_specs=pl.BlockSpec((tile, width), lambda i: (i, 0)),
        out_shape=jax.ShapeDtypeStruct((seq, width), BF16),
        scratch_shapes=[pltpu.VMEM((heads, tile, 1), F32),
                        pltpu.VMEM((heads, tile, HEAD_PAIR_WIDTH), F32)],
        compiler_params=pltpu.CompilerParams(
            dimension_semantics=("arbitrary",), vmem_limit_bytes=V7X_VMEM_LIMIT_BYTES),
        name="attn_prompt",
    )(q_bf, k_bf, v_bf)


def _attn_sample_kernel(q_ref, kn_ref, vn_ref, ck_hbm, cv_hbm, o_ref, kbuf, vbuf, sem, qs_ref, carry_ref, acc_ref,
                        *, layer, heads, past):
    n_batch, t_new, width = q_ref.shape
    q_rows = heads * t_new
    chunk = SAMPLE_CHUNK_TOKENS
    n_chunks = past // chunk
    lane = lax.broadcasted_iota(jnp.int32, (t_new, width), 1)
    in_head = [(lane >= h * SB_HEAD_DIM) & (lane < (h + 1) * SB_HEAD_DIM) for h in range(heads)]
    row = lax.broadcasted_iota(jnp.int32, (q_rows, t_new), 0)
    col = lax.broadcasted_iota(jnp.int32, (q_rows, t_new), 1)
    causal_new = col < (row % t_new)
    suffix_new = _suffix_ones(t_new)
    suffix = _suffix_ones(chunk)

    def chunk_copies(b, c, slot):
        tokens = pl.ds(pl.multiple_of(past - (c + 1) * chunk, chunk), chunk)
        return (pltpu.make_async_copy(ck_hbm.at[layer, b, :, :, tokens], kbuf.at[slot], sem.at[0, slot]),
                pltpu.make_async_copy(cv_hbm.at[layer, b, :, :, tokens], vbuf.at[slot], sem.at[1, slot]))

    def add_chunk(slot):
        k_t = kbuf[slot].reshape(width, chunk).astype(BF16)
        v_t = vbuf[slot].reshape(width, chunk).astype(BF16)
        pv, carry = _sb_chain([_dot(qs_ref[...], k_t)], [v_t], suffix, carry_ref[...], None, v_transposed=True)
        acc_ref[...] += pv
        carry_ref[...] = carry
        return jnp.min(carry)

    def one_batch(b, _):
        @pl.when(b + 1 < n_batch)
        def _():
            for cp in chunk_copies(b + 1, 0, (b + 1) % 2):
                cp.start()

        q = q_ref[b]
        for h in range(heads):
            qs_ref[h * t_new:(h + 1) * t_new, :] = jnp.where(in_head[h], q, jnp.zeros_like(q))
        pv, carry = _sb_chain([_dot_nt(qs_ref[...], kn_ref[b])], [vn_ref[b]], suffix_new,
                              jnp.zeros((q_rows, 1), F32), causal_new)
        acc_ref[...] = pv
        carry_ref[...] = carry
        for cp in chunk_copies(b, 0, b % 2):
            cp.wait()
        least = add_chunk(b % 2)

        def more(state):
            c, least = state
            return (c < n_chunks) & (least < F32_UNDERFLOW_BITS)

        def older_chunk(state):
            c, _ = state
            copies = chunk_copies(b, c, 2)
            for cp in copies:
                cp.start()
            for cp in copies:
                cp.wait()
            return c + 1, add_chunk(2)

        lax.while_loop(more, older_chunk, (1, least))
        out = jnp.zeros((t_new, width), F32)
        for h in range(heads):
            out = jnp.where(in_head[h], acc_ref[h * t_new:(h + 1) * t_new, :], out)
        o_ref[b] = out.astype(o_ref.dtype)
        return 0

    for cp in chunk_copies(0, 0, 0):
        cp.start()
    lax.fori_loop(0, n_batch, one_batch, 0)


def _attn_sample(q_bf, k_bf, v_bf, cache_k_t, cache_v_t, layer):
    n_batch, t_new, width = q_bf.shape
    heads, head_dim, past = cache_k_t.shape[2:]
    q_rows = heads * t_new
    full = lambda a: pl.BlockSpec(a.shape, lambda i: (0,) * a.ndim)
    chunk_buffers = pltpu.VMEM((3, heads, head_dim, SAMPLE_CHUNK_TOKENS), F32)
    return pl.pallas_call(
        functools.partial(_attn_sample_kernel, layer=layer, heads=heads, past=past),
        grid=(1,),
        in_specs=[full(q_bf), full(k_bf), full(v_bf),
                  pl.BlockSpec(memory_space=pl.ANY), pl.BlockSpec(memory_space=pl.ANY)],
        out_specs=full(q_bf),
        out_shape=jax.ShapeDtypeStruct(q_bf.shape, BF16),
        scratch_shapes=[chunk_buffers, chunk_buffers, pltpu.SemaphoreType.DMA((2, 3)),
                        pltpu.VMEM((q_rows, width), BF16),
                        pltpu.VMEM((q_rows, 1), F32),
                        pltpu.VMEM((q_rows, width), F32)],
        compiler_params=pltpu.CompilerParams(
            dimension_semantics=("arbitrary",), vmem_limit_bytes=V7X_VMEM_LIMIT_BYTES),
        name="attn_sample",
    )(q_bf, k_bf, v_bf, cache_k_t, cache_v_t)


def _trailing_window_sums(u_ext, axis):
    n = u_ext.shape[axis]
    t = n - HALO_ROWS
    sl = lambda a, lo, hi: lax.slice_in_dim(a, lo, hi, axis=axis)
    cols = lambda a, g: a[..., g * POOL_GROUP_WIDTH:]
    s2 = sl(u_ext, 1, n) + sl(u_ext, 0, n - 1)
    s2b = cols(s2, 1)
    s4 = sl(s2b, 2, n - 1) + sl(s2b, 0, n - 3)
    s4b = s4[..., POOL_GROUP_WIDTH:]
    s8 = sl(s4b, 4, n - 3) + sl(s4b, 0, n - 7)
    s8b = s8[..., POOL_GROUP_WIDTH:]
    s16 = sl(s8b, 8, n - 7) + sl(s8b, 0, n - 15)
    first = HALO_ROWS
    return (sl(s2, first - 1, first - 1 + t)[..., :POOL_GROUP_WIDTH],
            sl(s4, first - 3, first - 3 + t)[..., :POOL_GROUP_WIDTH],
            sl(s8, first - 7, first - 7 + t)[..., :POOL_GROUP_WIDTH],
            sl(s16, first - 15, first - 15 + t))


def _post_kernel(x_ref, u_ref, halo_ref, s_ref, mod_ref, wpool_ref, pscale_ref, wo_ref, g1_ref, b1_ref,
                 wgu_ref, wdown_ref, g2_ref, b2_ref, *refs, d_model, d_ff, alpha, ff_chunks, has_state, n_parts,
                 n_stack):
    kv_refs, (o_ref, *stacked_refs) = refs[:2 * n_stack], refs[2 * n_stack:]
    for l in range(n_stack):
        for src, dst in zip(kv_refs[2 * l:2 * l + 2], stacked_refs):
            dst[l] = src[...]
    mod = mod_ref[...]
    gate1 = mod[..., 2 * d_model:3 * d_model]
    shift2 = mod[..., 3 * d_model:4 * d_model]
    scale2 = mod[..., 4 * d_model:5 * d_model]
    gate2 = mod[..., 5 * d_model:6 * d_model]
    row_axis = x_ref.ndim - 2
    rows = x_ref.shape[row_axis] // n_parts
    assert has_state <= (n_parts == 1) and rows >= HALO_ROWS
    parts = [slice(p * rows, (p + 1) * rows) for p in range(n_parts)]
    in_part = lambda ref, part: ref[...] if has_state else ref[part, :]

    def pooled_mix_inputs(p):
        u = in_part(u_ref, parts[p])
        if has_state:
            halo = halo_ref[...]
            inv_counts = [1.0 / w for w in POOL_WINDOWS]
        else:
            if p == 0:
                halo = halo_ref[...]
                halo = jnp.where(pl.program_id(0) == 0, jnp.zeros_like(halo), halo)
            else:
                halo = u_ref[p * rows - HALO_ROWS:p * rows, :]
            t = (pl.program_id(0) * n_parts + p) * rows + lax.broadcasted_iota(jnp.int32, (rows, 1), 0)
            avail = (t + 1).astype(F32)
            inv_counts = [1.0 / jnp.minimum(avail, float(w)) for w in POOL_WINDOWS]
        sums = _trailing_window_sums(jnp.concatenate([halo, u], axis=row_axis), row_axis)
        pool_out = []
        for g, (win_sum, inv) in enumerate(zip(sums, inv_counts)):
            cur = u[..., g * POOL_GROUP_WIDTH:(g + 1) * POOL_GROUP_WIDTH]
            pooled = (win_sum * inv - cur).reshape(-1, POOL_GROUP_WIDTH)
            pool_out.append(_dot(pooled.astype(BF16), wpool_ref[g]))
        pool_out = jnp.concatenate(pool_out, axis=-1) * pscale_ref[...]
        sb_out = in_part(s_ref, parts[p])
        return pool_out.astype(BF16), sb_out.reshape(-1, sb_out.shape[-1]).astype(BF16)

    every = range(n_parts)
    xs = [in_part(x_ref, parts[p]) for p in every]
    mix_in = [pooled_mix_inputs(p) for p in every]
    mix = [_dot(pool_bf, wo_ref[0:POOL_WIDTH, :]) + _dot(sb_bf, wo_ref[POOL_WIDTH:, :]) for pool_bf, sb_bf in mix_in]
    x1 = [_layer_norm(alpha * xs[p] + gate1 * mix[p].reshape(xs[p].shape)) * g1_ref[...] + b1_ref[...] for p in every]
    h = [(_layer_norm(x1[p]) * (1.0 + scale2) + shift2).astype(BF16).reshape(-1, d_model) for p in every]
    ff = [None] * n_parts
    for lo, hi in ff_chunks:
        gate = [_dot(h[p], wgu_ref[:, lo:hi]) for p in every]
        up = [_dot(h[p], wgu_ref[:, d_ff + lo:d_ff + hi]) for p in every]
        act = [(_silu(gate[p]) * up[p]).astype(BF16) for p in every]
        part = [_dot(act[p], wdown_ref[lo:hi, :]) for p in every]
        ff = [part[p] if ff[p] is None else ff[p] + part[p] for p in every]
    for p in every:
        out = _layer_norm(alpha * x1[p] + gate2 * ff[p].reshape(xs[p].shape)) * g2_ref[...] + b2_ref[...]
        if has_state:
            o_ref[...] = out
        else:
            o_ref[parts[p], :] = out


def _ff_chunks(d_ff):
    mxu_cols = 256
    tiles = d_ff // mxu_cols
    assert tiles * mxu_cols == d_ff
    split = (tiles + 1) // 2 * mxu_cols
    return ((0, split), (split, d_ff))


def _post(x, u, halo_src, s, mod, layer, w_pool_bf, pool_scale, w_o_bf, ln1_g, ln1_b, w_gu_bf, w_down_bf,
          ln2_g, ln2_b, alpha, kv_layers=()):
    d_model = x.shape[-1]
    d_ff = w_down_bf.shape[1]
    depth = w_o_bf.shape[0]
    has_state = x.ndim == 3
    if has_state:
        grid = (1,)
        full3 = lambda a: pl.BlockSpec(a.shape, lambda i: (0, 0, 0))
        x_spec, u_spec, halo_spec, s_spec, mod_spec, out_spec = full3(x), full3(u), full3(halo_src), full3(s), full3(mod), full3(x)
    else:
        rows = x.shape[0]
        tm = ROW_TILE
        grid = (rows // tm,)
        halo_blocks = tm // HALO_ROWS
        x_spec = pl.BlockSpec((tm, d_model), lambda i: (i, 0))
        u_spec = pl.BlockSpec((tm, POOL_WIDTH), lambda i: (i, 0))
        halo_spec = pl.BlockSpec((HALO_ROWS, POOL_WIDTH), lambda i: (jnp.maximum(i * halo_blocks - 1, 0), 0))
        s_spec = pl.BlockSpec((tm, s.shape[-1]), lambda i: (i, 0))
        mod_spec = pl.BlockSpec(mod.shape, lambda i: (0, 0))
        out_spec = x_spec
    const = pl.Buffered(1)

    def layer_spec(a):
        nd = a.ndim - 1
        return pl.BlockSpec((None,) + a.shape[1:], lambda i: (layer,) + (0,) * nd, pipeline_mode=const)

    vec = lambda a: a.reshape(depth, 1, a.shape[-1])
    kv_flat = [a for pair in kv_layers for a in pair]
    out_specs, out_shape = [out_spec], [jax.ShapeDtypeStruct(x.shape, F32)]
    if kv_flat:
        sb = kv_flat[0].shape[0]
        kv_specs = [pl.BlockSpec((sb, tm), lambda i: (0, i))] * len(kv_flat)
        out_specs += [pl.BlockSpec((len(kv_layers), sb, tm), lambda i: (0, 0, i))] * 2
        out_shape += [jax.ShapeDtypeStruct((len(kv_layers),) + kv_flat[0].shape, F32)] * 2
    else:
        kv_specs = []
    return pl.pallas_call(
        functools.partial(_post_kernel, d_model=d_model, d_ff=d_ff, alpha=alpha, ff_chunks=_ff_chunks(d_ff),
                          has_state=has_state, n_parts=1 if has_state else POST_ROW_PARTS, n_stack=len(kv_layers)),
        grid=grid,
        in_specs=[x_spec, u_spec, halo_spec, s_spec, mod_spec,
                  layer_spec(w_pool_bf), layer_spec(vec(pool_scale)), layer_spec(w_o_bf),
                  layer_spec(vec(ln1_g)), layer_spec(vec(ln1_b)),
                  layer_spec(w_gu_bf), layer_spec(w_down_bf),
                  layer_spec(vec(ln2_g)), layer_spec(vec(ln2_b))] + kv_specs,
        out_specs=out_specs,
        out_shape=out_shape,
        compiler_params=pltpu.CompilerParams(
            dimension_semantics=("arbitrary",), vmem_limit_bytes=V7X_VMEM_LIMIT_BYTES),
        name="post",
    )(x, u, halo_src, s, mod, w_pool_bf, vec(pool_scale), w_o_bf, vec(ln1_g), vec(ln1_b),
      w_gu_bf, w_down_bf, vec(ln2_g), vec(ln2_b), *kv_flat)


def kernel(x_prompt, x_sample, cache_k, cache_v, state_pool, c_prompt, c_sample, w_ada, b_ada, w_in, w_pool,
           pool_scale, w_o, ln1_g, ln1_b, w_gu, w_down, ln2_g, ln2_b):
    depth, d_model, _ = w_ada.shape
    batch, seq, _ = x_prompt.shape
    dec_batch, dec_seq, _ = x_sample.shape
    past = cache_k.shape[2]
    heads, head_dim = cache_k.shape[3], cache_k.shape[4]
    assert batch == 1 and head_dim == SB_HEAD_DIM and state_pool.shape[2] == POOL_HIST
    assert seq % ROW_TILE == 0 and past % SAMPLE_CHUNK_TOKENS == 0 and dec_seq >= POOL_HIST
    alpha = (2 * depth) ** 0.25

    n_cond = batch + dec_batch
    c_all = jnp.concatenate([c_prompt, c_sample], axis=0)
    c_all = jnp.pad(c_all, ((0, -n_cond % 16), (0, 0)))
    mod = _adaln(c_all, w_ada, b_ada)

    w_in_bf, w_pool_bf, w_o_bf = w_in.astype(BF16), w_pool.astype(BF16), w_o.astype(BF16)
    n_uq = w_in.shape[-1] - 2 * heads * head_dim
    w_uq_bf = w_in_bf[:, :, :n_uq]
    w_kv_t_bf = w_in_bf[:, :, n_uq:].transpose(0, 2, 1)
    w_gu_bf, w_down_bf = w_gu.astype(BF16), w_down.astype(BF16)
    cache_k_t = cache_k.transpose(0, 1, 3, 4, 2)
    cache_v_t = cache_v.transpose(0, 1, 3, 4, 2)
    hist = jnp.pad(state_pool, ((0, 0), (0, 0), (HALO_ROWS - POOL_HIST, 0), (0, 0)))

    xp = x_prompt.reshape(seq, d_model)
    xs = x_sample
    kv_p, pool_p, k_s, v_s, pool_s = [], [], [], [], []
    for l in range(depth):
        post_w = (l, w_pool_bf, pool_scale, w_o_bf, ln1_g, ln1_b, w_gu_bf, w_down_bf, ln2_g, ln2_b, alpha)

        mod_p = mod[l, 0:batch]
        u, k, v, qb, kb, vb = _inproj_prompt(xp, mod_p, w_uq_bf, w_kv_t_bf, l)
        kv_p.append((k, v))
        s = _attn_prompt(qb, kb, vb)
        if l + 1 < depth:
            xp, = _post(xp, u, u, s, mod_p, *post_w)
        else:
            xp, k_p, v_p = _post(xp, u, u, s, mod_p, *post_w, kv_layers=kv_p)
        pool_p.append(u[seq - POOL_HIST:].reshape(batch, POOL_HIST, POOL_WIDTH))

        mod_s = mod[l, batch:n_cond].reshape(dec_batch, 1, -1)
        u, k, v, qb, kb, vb = _inproj(xs, mod_s, w_in_bf, l)
        to3 = lambda a: a.reshape(dec_batch, dec_seq, a.shape[-1])
        s = _attn_sample(to3(qb), to3(kb), to3(vb), cache_k_t, cache_v_t, l)
        u3 = to3(u)
        xs, = _post(xs, u3, hist[l], s, mod_s, *post_w)
        k_s.append(k.reshape(dec_batch, dec_seq, heads, head_dim))
        v_s.append(v.reshape(dec_batch, dec_seq, heads, head_dim))
        pool_s.append(u3[:, dec_seq - POOL_HIST:])

    by_token = lambda a: a.reshape(depth, batch, heads, head_dim, seq).transpose(0, 1, 4, 2, 3)
    return (xp.reshape(batch, seq, d_model), xs, by_token(k_p), by_token(v_p), jnp.stack(pool_p),
            jnp.stack(k_s), jnp.stack(v_s), jnp.stack(pool_s))
```

```python
import functools
import math

import jax
import jax.numpy as jnp
from jax import lax
from jax.experimental import pallas as pl
from jax.experimental.pallas import tpu as pltpu

F32 = jnp.float32
BF16 = jnp.bfloat16

POOL_WINDOWS = (2, 4, 8, 16)
POOL_GROUP_WIDTH = 128
POOL_WIDTH = POOL_GROUP_WIDTH * len(POOL_WINDOWS)
POOL_HIST = max(POOL_WINDOWS) - 1
HALO_ROWS = 16
SB_HEAD_DIM = 64
SB_SCALE = 1.0 / math.sqrt(SB_HEAD_DIM)
LOG2_E = 1.0 / math.log(2.0)
HEAD_PAIR_WIDTH = 2 * SB_HEAD_DIM
LN_EPS = 1e-5
F32_UNDERFLOW_BITS = 152.0
MASKED_LOGIT = -1e4

V7X_VMEM_LIMIT_BYTES = 56 * 1024 * 1024

ROW_TILE = 512
POST_ROW_PARTS = 2
ATTN_TILE = 256
SAMPLE_CHUNK_TOKENS = 256
ADA_COL_TILE = 1536


def _dot(a, b):
    return jnp.dot(a, b, preferred_element_type=F32)


def _dot_nt(a, b):
    return lax.dot_general(a, b, (((1,), (1,)), ((), ())), preferred_element_type=F32)


def _split_bf16(x):
    hi = x.astype(BF16)
    lo = (x - hi.astype(F32)).astype(BF16)
    return hi, lo


def _layer_norm(x):
    mu = jnp.mean(x, axis=-1, keepdims=True)
    xc = x - mu
    var = jnp.mean(xc * xc, axis=-1, keepdims=True)
    return xc * lax.rsqrt(var + LN_EPS)


def _silu(x):
    return x / (1.0 + jnp.exp(-x))


def _softplus2(z2):
    return jnp.maximum(z2, 0.0) + LOG2_E * jnp.log(1.0 + jnp.exp2(-jnp.abs(z2)))


def _suffix_ones(n):
    r = lax.broadcasted_iota(jnp.int32, (n, n), 0)
    c = lax.broadcasted_iota(jnp.int32, (n, n), 1)
    return jnp.where(r > c, 1.0, 0.0).astype(BF16)


def _adaln_kernel(c_ref, w_ref, b_ref, o_ref):
    cond = _silu(c_ref[...])
    c_hi, c_lo = _split_bf16(cond)
    w = w_ref[...].astype(BF16)
    o_ref[...] = _dot(c_hi, w) + _dot(c_lo, w) + b_ref[...]


def _adaln(c_all, w_ada, b_ada):
    depth, d_model, n_out = w_ada.shape
    rows = c_all.shape[0]
    tn = ADA_COL_TILE
    return pl.pallas_call(
        _adaln_kernel,
        grid=(depth, n_out // tn),
        in_specs=[
            pl.BlockSpec((rows, d_model), lambda l, j: (0, 0)),
            pl.BlockSpec((None, d_model, tn), lambda l, j: (l, 0, j)),
            pl.BlockSpec((None, 1, tn), lambda l, j: (l, 0, j)),
        ],
        out_specs=pl.BlockSpec((None, rows, tn), lambda l, j: (l, 0, j)),
        out_shape=jax.ShapeDtypeStruct((depth, rows, n_out), F32),
        compiler_params=pltpu.CompilerParams(
            dimension_semantics=("arbitrary", "arbitrary"), vmem_limit_bytes=V7X_VMEM_LIMIT_BYTES),
        name="adaln",
    )(c_all, w_ada, b_ada.reshape(depth, 1, n_out))


def _inproj_kernel(x_ref, mod_ref, w_ref, u_ref, k_ref, v_ref, qb_ref, kb_ref, vb_ref, *, d_model):
    x = x_ref[...]
    mod = mod_ref[...]
    shift = mod[..., 0:d_model]
    scale = mod[..., d_model:2 * d_model]
    h = (_layer_norm(x) * (1.0 + scale) + shift).astype(BF16)
    h = h.reshape(-1, d_model)
    proj = _dot(h, w_ref[...])
    sb = (proj.shape[-1] - POOL_WIDTH) // 3
    o = POOL_WIDTH
    u = proj[:, :o]
    q = proj[:, o:o + sb]
    k = proj[:, o + sb:o + 2 * sb]
    v = proj[:, o + 2 * sb:o + 3 * sb]
    u_ref[...] = u
    k_ref[...] = k
    v_ref[...] = v
    qb_ref[...] = (q * (SB_SCALE * LOG2_E)).astype(BF16)
    kb_ref[...] = k.astype(BF16)
    vb_ref[...] = v.astype(BF16)


def _inproj(x, mod, w_in_bf, layer):
    d_model = x.shape[-1]
    n_out = w_in_bf.shape[-1]
    sb = (n_out - POOL_WIDTH) // 3
    rows = x.shape[0] * x.shape[1]
    row_spec = lambda w: pl.BlockSpec((rows, w), lambda i: (0, 0))
    return pl.pallas_call(
        functools.partial(_inproj_kernel, d_model=d_model),
        grid=(1,),
        in_specs=[pl.BlockSpec(x.shape, lambda i: (0, 0, 0)), pl.BlockSpec(mod.shape, lambda i: (0, 0, 0)),
                  pl.BlockSpec((None, d_model, n_out), lambda i: (layer, 0, 0))],
        out_specs=[row_spec(POOL_WIDTH), row_spec(sb), row_spec(sb), row_spec(sb), row_spec(sb), row_spec(sb)],
        out_shape=[jax.ShapeDtypeStruct((rows, POOL_WIDTH), F32),
                   jax.ShapeDtypeStruct((rows, sb), F32),
                   jax.ShapeDtypeStruct((rows, sb), F32),
                   jax.ShapeDtypeStruct((rows, sb), BF16),
                   jax.ShapeDtypeStruct((rows, sb), BF16),
                   jax.ShapeDtypeStruct((rows, sb), BF16)],
        compiler_params=pltpu.CompilerParams(
            dimension_semantics=("arbitrary",), vmem_limit_bytes=V7X_VMEM_LIMIT_BYTES),
        name="inproj",
    )(x, mod, w_in_bf)


def _inproj_prompt_kernel(x_ref, mod_ref, w_uq_ref, w_kv_t_ref, u_ref, k_ref, v_ref, qb_ref, kb_ref, vb_ref,
                          *, d_model, key_block):
    mod = mod_ref[...]
    shift = mod[..., 0:d_model]
    scale = mod[..., d_model:2 * d_model]
    parts = [slice(c * key_block, (c + 1) * key_block) for c in range(x_ref.shape[0] // key_block)]
    hs = [(_layer_norm(x_ref[part, :]) * (1.0 + scale) + shift).astype(BF16) for part in parts]
    uqs = [_dot(h, w_uq_ref[...]) for h in hs]
    kv_ts = [_dot_nt(w_kv_t_ref[...], h) for h in hs]
    sb = w_kv_t_ref.shape[0] // 2
    for c, (part, uq, kv_t) in enumerate(zip(parts, uqs, kv_ts)):
        u_ref[part, :] = uq[:, :POOL_WIDTH]
        qb_ref[part, :] = (uq[:, POOL_WIDTH:] * (SB_SCALE * LOG2_E)).astype(BF16)
        for f32_ref, bf_ref, val in ((k_ref, kb_ref, kv_t[:sb]), (v_ref, vb_ref, kv_t[sb:])):
            f32_ref[:, part] = val
            bf_ref[c] = val.astype(BF16)


def _inproj_prompt(x, mod, w_uq_bf, w_kv_t_bf, layer):
    seq, d_model = x.shape
    n_uq = w_uq_bf.shape[-1]
    sb = w_kv_t_bf.shape[1] // 2
    tm = ROW_TILE
    key_block = ATTN_TILE
    row_spec = lambda w: pl.BlockSpec((tm, w), lambda i: (i, 0))
    t_spec = pl.BlockSpec((sb, tm), lambda i: (0, i))
    blocks_spec = pl.BlockSpec((tm // key_block, sb, key_block), lambda i: (i, 0, 0))
    return pl.pallas_call(
        functools.partial(_inproj_prompt_kernel, d_model=d_model, key_block=key_block),
        grid=(seq // tm,),
        in_specs=[row_spec(d_model), pl.BlockSpec(mod.shape, lambda i: (0, 0)),
                  pl.BlockSpec((None, d_model, n_uq), lambda i: (layer, 0, 0)),
                  pl.BlockSpec((None, 2 * sb, d_model), lambda i: (layer, 0, 0))],
        out_specs=[row_spec(POOL_WIDTH), t_spec, t_spec, row_spec(sb), blocks_spec, blocks_spec],
        out_shape=[jax.ShapeDtypeStruct((seq, POOL_WIDTH), F32),
                   jax.ShapeDtypeStruct((sb, seq), F32),
                   jax.ShapeDtypeStruct((sb, seq), F32),
                   jax.ShapeDtypeStruct((seq, sb), BF16),
                   jax.ShapeDtypeStruct((seq // key_block, sb, key_block), BF16),
                   jax.ShapeDtypeStruct((seq // key_block, sb, key_block), BF16)],
        compiler_params=pltpu.CompilerParams(
            dimension_semantics=("arbitrary",), vmem_limit_bytes=V7X_VMEM_LIMIT_BYTES),
        name="inproj_prompt",
    )(x, mod, w_uq_bf, w_kv_t_bf)


def _sb_bits(z2s, suffix):
    log_betas, later, sums = [], [], []
    for z2 in z2s:
        bits = _softplus2(z2)
        log_betas.append(z2 - bits)
        later.append(_dot(bits.astype(BF16), suffix))
        sums.append(later[-1][:, :1] + bits[:, :1])
    return log_betas, later, sums


def _sb_weighted(log_betas, later, carries, v_bfs, v_transposed=False):
    pv_dot = _dot_nt if v_transposed else _dot
    return [pv_dot(jnp.exp2(lb - (l + c)).astype(BF16), v) for lb, l, c, v in zip(log_betas, later, carries, v_bfs)]


def _masked(z2s, mask):
    return z2s if mask is None else [jnp.where(mask, z2, MASKED_LOGIT) for z2 in z2s]


def _sb_blocks(z2s, v_bfs, suffix, carries, mask, v_transposed=False):
    log_betas, later, sums = _sb_bits(_masked(z2s, mask), suffix)
    return _sb_weighted(log_betas, later, carries, v_bfs, v_transposed), sums


def _sb_chain(z2s, v_bfs, suffix, carry, mask, v_transposed=False):
    log_betas, later, sums = _sb_bits(_masked(z2s, mask), suffix)
    carries = [carry]
    for block_sum in sums:
        carries.append(carries[-1] + block_sum)
    pvs = _sb_weighted(log_betas, later, carries[:-1], v_bfs, v_transposed)
    total = pvs[0]
    for pv in pvs[1:]:
        total = total + pv
    return total, carries[-1]


def _attn_prompt_kernel(q_ref, k_ref, v_ref, o_ref, carry_ref, acc_ref, *, tile, heads):
    i = pl.program_id(0)
    lane = lax.broadcasted_iota(jnp.int32, (tile, HEAD_PAIR_WIDTH), 1)
    low = lane < SB_HEAD_DIM
    q_heads = []
    for h in range(heads):
        qp = q_ref[:, (h // 2) * HEAD_PAIR_WIDTH:(h // 2 + 1) * HEAD_PAIR_WIDTH]
        q_heads.append(jnp.where(low if h % 2 == 0 else ~low, qp, jnp.zeros_like(qp)))
    suffix = _suffix_ones(tile)
    row = lax.broadcasted_iota(jnp.int32, (tile, tile), 0)
    col = lax.broadcasted_iota(jnp.int32, (tile, tile), 1)
    causal = col < row

    def block(j, mask, first):
        pair_rows = lambda h: slice((h // 2) * HEAD_PAIR_WIDTH, (h // 2 + 1) * HEAD_PAIR_WIDTH)
        z2s = [_dot(q_heads[h], k_ref[j, pair_rows(h), :]) for h in range(heads)]
        v_blks = [v_ref[j, pair_rows(h), :] for h in range(heads)]
        carries = [jnp.zeros((tile, 1), F32) if first else carry_ref[h] for h in range(heads)]
        pvs, blk_bits = _sb_blocks(z2s, v_blks, suffix, carries, mask, v_transposed=True)
        least = None
        for h in range(heads):
            carry = carries[h] + blk_bits[h]
            carry_ref[h] = carry
            acc_ref[h] = pvs[h] if first else acc_ref[h] + pvs[h]
            least = carry if least is None else jnp.minimum(least, carry)
        return jnp.min(least)

    least = block(i, causal, True)

    def more(state):
        j, least = state
        return (j >= 0) & (least < F32_UNDERFLOW_BITS)

    lax.while_loop(more, lambda state: (state[0] - 1, block(state[0], None, False)), (i - 1, least))
    for p in range(heads // 2):
        out = jnp.where(low, acc_ref[2 * p], acc_ref[2 * p + 1])
        o_ref[:, p * HEAD_PAIR_WIDTH:(p + 1) * HEAD_PAIR_WIDTH] = out.astype(o_ref.dtype)


def _attn_prompt(q_bf, k_bf, v_bf):
    seq, width = q_bf.shape
    tile = ATTN_TILE
    heads = width // SB_HEAD_DIM
    resident = lambda: pl.BlockSpec(k_bf.shape, lambda i: (0, 0, 0), pipeline_mode=pl.Buffered(1))
    return pl.pallas_call(
        functools.partial(_attn_prompt_kernel, tile=tile, heads=heads),
        grid=(seq // tile,),
        in_specs=[pl.BlockSpec((tile, width), lambda i: (i, 0)), resident(), resident()],
        out_specs=pl.BlockSpec((tile, width), lambda i: (i, 0)),
        out_shape=jax.ShapeDtypeStruct((seq, width), BF16),
        scratch_shapes=[pltpu.VMEM((heads, tile, 1), F32),
                        pltpu.VMEM((heads, tile, HEAD_PAIR_WIDTH), F32)],
        compiler_params=pltpu.CompilerParams(
            dimension_semantics=("arbitrary",), vmem_limit_bytes=V7X_VMEM_LIMIT_BYTES),
        name="attn_prompt",
    )(q_bf, k_bf, v_bf)


def _attn_sample_kernel(q_ref, kn_ref, vn_ref, ck_hbm, cv_hbm, o_ref, kbuf, vbuf, sem, qs_ref, carry_ref, acc_ref,
                        *, layer, heads, past):
    n_batch, t_new, width = q_ref.shape
    q_rows = heads * t_new
    chunk = SAMPLE_CHUNK_TOKENS
    n_chunks = past // chunk
    lane = lax.broadcasted_iota(jnp.int32, (t_new, width), 1)
    in_head = [(lane >= h * SB_HEAD_DIM) & (lane < (h + 1) * SB_HEAD_DIM) for h in range(heads)]
    row = lax.broadcasted_iota(jnp.int32, (q_rows, t_new), 0)
    col = lax.broadcasted_iota(jnp.int32, (q_rows, t_new), 1)
    causal_new = col < (row % t_new)
    suffix_new = _suffix_ones(t_new)
    suffix = _suffix_ones(chunk)

    def chunk_copies(b, c, slot):
        tokens = pl.ds(pl.multiple_of(past - (c + 1) * chunk, chunk), chunk)
        return (pltpu.make_async_copy(ck_hbm.at[layer, b, :, :, tokens], kbuf.at[slot], sem.at[0, slot]),
                pltpu.make_async_copy(cv_hbm.at[layer, b, :, :, tokens], vbuf.at[slot], sem.at[1, slot]))

    def add_chunk(slot):
        k_t = kbuf[slot].reshape(width, chunk).astype(BF16)
        v_t = vbuf[slot].reshape(width, chunk).astype(BF16)
        pv, carry = _sb_chain([_dot(qs_ref[...], k_t)], [v_t], suffix, carry_ref[...], None, v_transposed=True)
        acc_ref[...] += pv
        carry_ref[...] = carry
        return jnp.min(carry)

    def one_batch(b, _):
        @pl.when(b + 1 < n_batch)
        def _():
            for cp in chunk_copies(b + 1, 0, (b + 1) % 2):
                cp.start()

        q = q_ref[b]
        for h in range(heads):
            qs_ref[h * t_new:(h + 1) * t_new, :] = jnp.where(in_head[h], q, jnp.zeros_like(q))
        pv, carry = _sb_chain([_dot_nt(qs_ref[...], kn_ref[b])], [vn_ref[b]], suffix_new,
                              jnp.zeros((q_rows, 1), F32), causal_new)
        acc_ref[...] = pv
        carry_ref[...] = carry
        for cp in chunk_copies(b, 0, b % 2):
            cp.wait()
        least = add_chunk(b % 2)

        def more(state):
            c, least = state
            return (c < n_chunks) & (least < F32_UNDERFLOW_BITS)

        def older_chunk(state):
            c, _ = state
            copies = chunk_copies(b, c, 2)
            for cp in copies:
                cp.start()
            for cp in copies:
                cp.wait()
            return c + 1, add_chunk(2)

        lax.while_loop(more, older_chunk, (1, least))
        out = jnp.zeros((t_new, width), F32)
        for h in range(heads):
            out = jnp.where(in_head[h], acc_ref[h * t_new:(h + 1) * t_new, :], out)
        o_ref[b] = out.astype(o_ref.dtype)
        return 0

    for cp in chunk_copies(0, 0, 0):
        cp.start()
    lax.fori_loop(0, n_batch, one_batch, 0)


def _attn_sample(q_bf, k_bf, v_bf, cache_k_t, cache_v_t, layer):
    n_batch, t_new, width = q_bf.shape
    heads, head_dim, past = cache_k_t.shape[2:]
    q_rows = heads * t_new
    full = lambda a: pl.BlockSpec(a.shape, lambda i: (0,) * a.ndim)
    chunk_buffers = pltpu.VMEM((3, heads, head_dim, SAMPLE_CHUNK_TOKENS), F32)
    return pl.pallas_call(
        functools.partial(_attn_sample_kernel, layer=layer, heads=heads, past=past),
        grid=(1,),
        in_specs=[full(q_bf), full(k_bf), full(v_bf),
                  pl.BlockSpec(memory_space=pl.ANY), pl.BlockSpec(memory_space=pl.ANY)],
        out_specs=full(q_bf),
        out_shape=jax.ShapeDtypeStruct(q_bf.shape, BF16),
        scratch_shapes=[chunk_buffers, chunk_buffers, pltpu.SemaphoreType.DMA((2, 3)),
                        pltpu.VMEM((q_rows, width), BF16),
                        pltpu.VMEM((q_rows, 1), F32),
                        pltpu.VMEM((q_rows, width), F32)],
        compiler_params=pltpu.CompilerParams(
            dimension_semantics=("arbitrary",), vmem_limit_bytes=V7X_VMEM_LIMIT_BYTES),
        name="attn_sample",
    )(q_bf, k_bf, v_bf, cache_k_t, cache_v_t)


def _trailing_window_sums(u_ext, axis):
    n = u_ext.shape[axis]
    t = n - HALO_ROWS
    sl = lambda a, lo, hi: lax.slice_in_dim(a, lo, hi, axis=axis)
    cols = lambda a, g: a[..., g * POOL_GROUP_WIDTH:]
    s2 = sl(u_ext, 1, n) + sl(u_ext, 0, n - 1)
    s2b = cols(s2, 1)
    s4 = sl(s2b, 2, n - 1) + sl(s2b, 0, n - 3)
    s4b = s4[..., POOL_GROUP_WIDTH:]
    s8 = sl(s4b, 4, n - 3) + sl(s4b, 0, n - 7)
    s8b = s8[..., POOL_GROUP_WIDTH:]
    s16 = sl(s8b, 8, n - 7) + sl(s8b, 0, n - 15)
    first = HALO_ROWS
    return (sl(s2, first - 1, first - 1 + t)[..., :POOL_GROUP_WIDTH],
            sl(s4, first - 3, first - 3 + t)[..., :POOL_GROUP_WIDTH],
            sl(s8, first - 7, first - 7 + t)[..., :POOL_GROUP_WIDTH],
            sl(s16, first - 15, first - 15 + t))


def _post_kernel(x_ref, u_ref, halo_ref, s_ref, mod_ref, wpool_ref, pscale_ref, wo_ref, g1_ref, b1_ref,
                 wgu_ref, wdown_ref, g2_ref, b2_ref, *refs, d_model, d_ff, alpha, ff_chunks, has_state, n_parts,
                 n_stack):
    kv_refs, (o_ref, *stacked_refs) = refs[:2 * n_stack], refs[2 * n_stack:]
    for l in range(n_stack):
        for src, dst in zip(kv_refs[2 * l:2 * l + 2], stacked_refs):
            dst[l] = src[...]
    mod = mod_ref[...]
    gate1 = mod[..., 2 * d_model:3 * d_model]
    shift2 = mod[..., 3 * d_model:4 * d_model]
    scale2 = mod[..., 4 * d_model:5 * d_model]
    gate2 = mod[..., 5 * d_model:6 * d_model]
    row_axis = x_ref.ndim - 2
    rows = x_ref.shape[row_axis] // n_parts
    assert has_state <= (n_parts == 1) and rows >= HALO_ROWS
    parts = [slice(p * rows, (p + 1) * rows) for p in range(n_parts)]
    in_part = lambda ref, part: ref[...] if has_state else ref[part, :]

    def pooled_mix_inputs(p):
        u = in_part(u_ref, parts[p])
        if has_state:
            halo = halo_ref[...]
            inv_counts = [1.0 / w for w in POOL_WINDOWS]
        else:
            if p == 0:
                halo = halo_ref[...]
                halo = jnp.where(pl.program_id(0) == 0, jnp.zeros_like(halo), halo)
            else:
                halo = u_ref[p * rows - HALO_ROWS:p * rows, :]
            t = (pl.program_id(0) * n_parts + p) * rows + lax.broadcasted_iota(jnp.int32, (rows, 1), 0)
            avail = (t + 1).astype(F32)
            inv_counts = [1.0 / jnp.minimum(avail, float(w)) for w in POOL_WINDOWS]
        sums = _trailing_window_sums(jnp.concatenate([halo, u], axis=row_axis), row_axis)
        pool_out = []
        for g, (win_sum, inv) in enumerate(zip(sums, inv_counts)):
            cur = u[..., g * POOL_GROUP_WIDTH:(g + 1) * POOL_GROUP_WIDTH]
            pooled = (win_sum * inv - cur).reshape(-1, POOL_GROUP_WIDTH)
            pool_out.append(_dot(pooled.astype(BF16), wpool_ref[g]))
        pool_out = jnp.concatenate(pool_out, axis=-1) * pscale_ref[...]
        sb_out = in_part(s_ref, parts[p])
        return pool_out.astype(BF16), sb_out.reshape(-1, sb_out.shape[-1]).astype(BF16)

    every = range(n_parts)
    xs = [in_part(x_ref, parts[p]) for p in every]
    mix_in = [pooled_mix_inputs(p) for p in every]
    mix = [_dot(pool_bf, wo_ref[0:POOL_WIDTH, :]) + _dot(sb_bf, wo_ref[POOL_WIDTH:, :]) for pool_bf, sb_bf in mix_in]
    x1 = [_layer_norm(alpha * xs[p] + gate1 * mix[p].reshape(xs[p].shape)) * g1_ref[...] + b1_ref[...] for p in every]
    h = [(_layer_norm(x1[p]) * (1.0 + scale2) + shift2).astype(BF16).reshape(-1, d_model) for p in every]
    ff = [None] * n_parts
    for lo, hi in ff_chunks:
        gate = [_dot(h[p], wgu_ref[:, lo:hi]) for p in every]
        up = [_dot(h[p], wgu_ref[:, d_ff + lo:d_ff + hi]) for p in every]
        act = [(_silu(gate[p]) * up[p]).astype(BF16) for p in every]
        part = [_dot(act[p], wdown_ref[lo:hi, :]) for p in every]
        ff = [part[p] if ff[p] is None else ff[p] + part[p] for p in every]
    for p in every:
        out = _layer_norm(alpha * x1[p] + gate2 * ff[p].reshape(xs[p].shape)) * g2_ref[...] + b2_ref[...]
        if has_state:
            o_ref[...] = out
        else:
            o_ref[parts[p], :] = out


def _ff_chunks(d_ff):
    mxu_cols = 256
    tiles = d_ff // mxu_cols
    assert tiles * mxu_cols == d_ff
    split = (tiles + 1) // 2 * mxu_cols
    return ((0, split), (split, d_ff))


def _post(x, u, halo_src, s, mod, layer, w_pool_bf, pool_scale, w_o_bf, ln1_g, ln1_b, w_gu_bf, w_down_bf,
          ln2_g, ln2_b, alpha, kv_layers=()):
    d_model = x.shape[-1]
    d_ff = w_down_bf.shape[1]
    depth = w_o_bf.shape[0]
    has_state = x.ndim == 3
    if has_state:
        grid = (1,)
        full3 = lambda a: pl.BlockSpec(a.shape, lambda i: (0, 0, 0))
        x_spec, u_spec, halo_spec, s_spec, mod_spec, out_spec = full3(x), full3(u), full3(halo_src), full3(s), full3(mod), full3(x)
    else:
        rows = x.shape[0]
        tm = ROW_TILE
        grid = (rows // tm,)
        halo_blocks = tm // HALO_ROWS
        x_spec = pl.BlockSpec((tm, d_model), lambda i: (i, 0))
        u_spec = pl.BlockSpec((tm, POOL_WIDTH), lambda i: (i, 0))
        halo_spec = pl.BlockSpec((HALO_ROWS, POOL_WIDTH), lambda i: (jnp.maximum(i * halo_blocks - 1, 0), 0))
        s_spec = pl.BlockSpec((tm, s.shape[-1]), lambda i: (i, 0))
        mod_spec = pl.BlockSpec(mod.shape, lambda i: (0, 0))
        out_spec = x_spec
    const = pl.Buffered(1)

    def layer_spec(a):
        nd = a.ndim - 1
        return pl.BlockSpec((None,) + a.shape[1:], lambda i: (layer,) + (0,) * nd, pipeline_mode=const)

    vec = lambda a: a.reshape(depth, 1, a.shape[-1])
    kv_flat = [a for pair in kv_layers for a in pair]
    out_specs, out_shape = [out_spec], [jax.ShapeDtypeStruct(x.shape, F32)]
    if kv_flat:
        sb = kv_flat[0].shape[0]
        kv_specs = [pl.BlockSpec((sb, tm), lambda i: (0, i))] * len(kv_flat)
        out_specs += [pl.BlockSpec((len(kv_layers), sb, tm), lambda i: (0, 0, i))] * 2
        out_shape += [jax.ShapeDtypeStruct((len(kv_layers),) + kv_flat[0].shape, F32)] * 2
    else:
        kv_specs = []
    return pl.pallas_call(
        functools.partial(_post_kernel, d_model=d_model, d_ff=d_ff, alpha=alpha, ff_chunks=_ff_chunks(d_ff),
                          has_state=has_state, n_parts=1 if has_state else POST_ROW_PARTS, n_stack=len(kv_layers)),
        grid=grid,
        in_specs=[x_spec, u_spec, halo_spec, s_spec, mod_spec,
                  layer_spec(w_pool_bf), layer_spec(vec(pool_scale)), layer_spec(w_o_bf),
                  layer_spec(vec(ln1_g)), layer_spec(vec(ln1_b)),
                  layer_spec(w_gu_bf), layer_spec(w_down_bf),
                  layer_spec(vec(ln2_g)), layer_spec(vec(ln2_b))] + kv_specs,
        out_specs=out_specs,
        out_shape=out_shape,
        compiler_params=pltpu.CompilerParams(
            dimension_semantics=("arbitrary",), vmem_limit_bytes=V7X_VMEM_LIMIT_BYTES),
        name="post",
    )(x, u, halo_src, s, mod, w_pool_bf, vec(pool_scale), w_o_bf, vec(ln1_g), vec(ln1_b),
      w_gu_bf, w_down_bf, vec(ln2_g), vec(ln2_b), *kv_flat)


def kernel(x_prompt, x_sample, cache_k, cache_v, state_pool, c_prompt, c_sample, w_ada, b_ada, w_in, w_pool,
           pool_scale, w_o, ln1_g, ln1_b, w_gu, w_down, ln2_g, ln2_b):
    depth, d_model, _ = w_ada.shape
    batch, seq, _ = x_prompt.shape
    dec_batch, dec_seq, _ = x_sample.shape
    past = cache_k.shape[2]
    heads, head_dim = cache_k.shape[3], cache_k.shape[4]
    assert batch == 1 and head_dim == SB_HEAD_DIM and state_pool.shape[2] == POOL_HIST
    assert seq % ROW_TILE == 0 and past % SAMPLE_CHUNK_TOKENS == 0 and dec_seq >= POOL_HIST
    alpha = (2 * depth) ** 0.25

    n_cond = batch + dec_batch
    c_all = jnp.concatenate([c_prompt, c_sample], axis=0)
    c_all = jnp.pad(c_all, ((0, -n_cond % 16), (0, 0)))
    mod = _adaln(c_all, w_ada, b_ada)

    w_in_bf, w_pool_bf, w_o_bf = w_in.astype(BF16), w_pool.astype(BF16), w_o.astype(BF16)
    n_uq = w_in.shape[-1] - 2 * heads * head_dim
    w_uq_bf = w_in_bf[:, :, :n_uq]
    w_kv_t_bf = w_in_bf[:, :, n_uq:].transpose(0, 2, 1)
    w_gu_bf, w_down_bf = w_gu.astype(BF16), w_down.astype(BF16)
    cache_k_t = cache_k.transpose(0, 1, 3, 4, 2)
    cache_v_t = cache_v.transpose(0, 1, 3, 4, 2)
    hist = jnp.pad(state_pool, ((0, 0), (0, 0), (HALO_ROWS - POOL_HIST, 0), (0, 0)))

    xp = x_prompt.reshape(seq, d_model)
    xs = x_sample
    kv_p, pool_p, k_s, v_s, pool_s = [], [], [], [], []
    for l in range(depth):
        post_w = (l, w_pool_bf, pool_scale, w_o_bf, ln1_g, ln1_b, w_gu_bf, w_down_bf, ln2_g, ln2_b, alpha)

        mod_p = mod[l, 0:batch]
        u, k, v, qb, kb, vb = _inproj_prompt(xp, mod_p, w_uq_bf, w_kv_t_bf, l)
        kv_p.append((k, v))
        s = _attn_prompt(qb, kb, vb)
        if l + 1 < depth:
            xp, = _post(xp, u, u, s, mod_p, *post_w)
        else:
            xp, k_p, v_p = _post(xp, u, u, s, mod_p, *post_w, kv_layers=kv_p)
        pool_p.append(u[seq - POOL_HIST:].reshape(batch, POOL_HIST, POOL_WIDTH))

        mod_s = mod[l, batch:n_cond].reshape(dec_batch, 1, -1)
        u, k, v, qb, kb, vb = _inproj(xs, mod_s, w_in_bf, l)
        to3 = lambda a: a.reshape(dec_batch, dec_seq, a.shape[-1])
        s = _attn_sample(to3(qb), to3(kb), to3(vb), cache_k_t, cache_v_t, l)
        u3 = to3(u)
        xs, = _post(xs, u3, hist[l], s, mod_s, *post_w)
        k_s.append(k.reshape(dec_batch, dec_seq, heads, head_dim))
        v_s.append(v.reshape(dec_batch, dec_seq, heads, head_dim))
        pool_s.append(u3[:, dec_seq - POOL_HIST:])

    by_token = lambda a: a.reshape(depth, batch, heads, head_dim, seq).transpose(0, 1, 4, 2, 3)
    return (xp.reshape(batch, seq, d_model), xs, by_token(k_p), by_token(v_p), jnp.stack(pool_p),
            jnp.stack(k_s), jnp.stack(v_s), jnp.stack(pool_s))
```

```python
import functools
import math

import jax
import jax.numpy as jnp
from jax import lax
from jax.experimental import pallas as pl
from jax.experimental.pallas import tpu as pltpu

F32 = jnp.float32
BF16 = jnp.bfloat16

POOL_WINDOWS = (2, 4, 8, 16)
POOL_GROUP_WIDTH = 128
POOL_WIDTH = POOL_GROUP_WIDTH * len(POOL_WINDOWS)
POOL_HIST = max(POOL_WINDOWS) - 1
HALO_ROWS = 16
SB_HEAD_DIM = 64
SB_SCALE = 1.0 / math.sqrt(SB_HEAD_DIM)
LOG2_E = 1.0 / math.log(2.0)
HEAD_PAIR_WIDTH = 2 * SB_HEAD_DIM
LN_EPS = 1e-5
F32_UNDERFLOW_BITS = 152.0
MASKED_LOGIT = -1e4

V7X_VMEM_LIMIT_BYTES = 56 * 1024 * 1024

ROW_TILE = 512
POST_ROW_PARTS = 2
ATTN_TILE = 256
SAMPLE_CHUNK_TOKENS = 256
ADA_COL_TILE = 1536


def _dot(a, b):
    return jnp.dot(a, b, preferred_element_type=F32)


def _dot_nt(a, b):
    return lax.dot_general(a, b, (((1,), (1,)), ((), ())), preferred_element_type=F32)


def _split_bf16(x):
    hi = x.astype(BF16)
    lo = (x - hi.astype(F32)).astype(BF16)
    return hi, lo


def _layer_norm(x):
    mu = jnp.mean(x, axis=-1, keepdims=True)
    xc = x - mu
    var = jnp.mean(xc * xc, axis=-1, keepdims=True)
    return xc * lax.rsqrt(var + LN_EPS)


def _silu(x):
    return x / (1.0 + jnp.exp(-x))


def _softplus2(z2):
    return jnp.maximum(z2, 0.0) + LOG2_E * jnp.log(1.0 + jnp.exp2(-jnp.abs(z2)))


def _suffix_ones(n):
    r = lax.broadcasted_iota(jnp.int32, (n, n), 0)
    c = lax.broadcasted_iota(jnp.int32, (n, n), 1)
    return jnp.where(r > c, 1.0, 0.0).astype(BF16)


def _adaln_kernel(c_ref, w_ref, b_ref, o_ref):
    cond = _silu(c_ref[...])
    c_hi, c_lo = _split_bf16(cond)
    w = w_ref[...].astype(BF16)
    o_ref[...] = _dot(c_hi, w) + _dot(c_lo, w) + b_ref[...]


def _adaln(c_all, w_ada, b_ada):
    depth, d_model, n_out = w_ada.shape
    rows = c_all.shape[0]
    tn = ADA_COL_TILE
    return pl.pallas_call(
        _adaln_kernel,
        grid=(depth, n_out // tn),
        in_specs=[
            pl.BlockSpec((rows, d_model), lambda l, j: (0, 0)),
            pl.BlockSpec((None, d_model, tn), lambda l, j: (l, 0, j)),
            pl.BlockSpec((None, 1, tn), lambda l, j: (l, 0, j)),
        ],
        out_specs=pl.BlockSpec((None, rows, tn), lambda l, j: (l, 0, j)),
        out_shape=jax.ShapeDtypeStruct((depth, rows, n_out), F32),
        compiler_params=pltpu.CompilerParams(
            dimension_semantics=("arbitrary", "arbitrary"), vmem_limit_bytes=V7X_VMEM_LIMIT_BYTES),
        name="adaln",
    )(c_all, w_ada, b_ada.reshape(depth, 1, n_out))


def _inproj_kernel(x_ref, mod_ref, w_ref, u_ref, k_ref, v_ref, qb_ref, kb_ref, vb_ref, *, d_model):
    x = x_ref[...]
    mod = mod_ref[...]
    shift = mod[..., 0:d_model]
    scale = mod[..., d_model:2 * d_model]
    h = (_layer_norm(x) * (1.0 + scale) + shift).astype(BF16)
    h = h.reshape(-1, d_model)
    proj = _dot(h, w_ref[...])
    sb = (proj.shape[-1] - POOL_WIDTH) // 3
    o = POOL_WIDTH
    u = proj[:, :o]
    q = proj[:, o:o + sb]
    k = proj[:, o + sb:o + 2 * sb]
    v = proj[:, o + 2 * sb:o + 3 * sb]
    u_ref[...] = u
    k_ref[...] = k
    v_ref[...] = v
    qb_ref[...] = (q * (SB_SCALE * LOG2_E)).astype(BF16)
    kb_ref[...] = k.astype(BF16)
    vb_ref[...] = v.astype(BF16)


def _inproj(x, mod, w_in_bf, layer):
    d_model = x.shape[-1]
    n_out = w_in_bf.shape[-1]
    sb = (n_out - POOL_WIDTH) // 3
    rows = x.shape[0] * x.shape[1]
    row_spec = lambda w: pl.BlockSpec((rows, w), lambda i: (0, 0))
    return pl.pallas_call(
        functools.partial(_inproj_kernel, d_model=d_model),
        grid=(1,),
        in_specs=[pl.BlockSpec(x.shape, lambda i: (0, 0, 0)), pl.BlockSpec(mod.shape, lambda i: (0, 0, 0)),
                  pl.BlockSpec((None, d_model, n_out), lambda i: (layer, 0, 0))],
        out_specs=[row_spec(POOL_WIDTH), row_spec(sb), row_spec(sb), row_spec(sb), row_spec(sb), row_spec(sb)],
        out_shape=[jax.ShapeDtypeStruct((rows, POOL_WIDTH), F32),
                   jax.ShapeDtypeStruct((rows, sb), F32),
                   jax.ShapeDtypeStruct((rows, sb), F32),
                   jax.ShapeDtypeStruct((rows, sb), BF16),
                   jax.ShapeDtypeStruct((rows, sb), BF16),
                   jax.ShapeDtypeStruct((rows, sb), BF16)],
        compiler_params=pltpu.CompilerParams(
            dimension_semantics=("arbitrary",), vmem_limit_bytes=V7X_VMEM_LIMIT_BYTES),
        name="inproj",
    )(x, mod, w_in_bf)


def _inproj_prompt_kernel(x_ref, mod_ref, w_uq_ref, w_kv_t_ref, u_ref, k_ref, v_ref, qb_ref, kb_ref, vb_ref,
                          *, d_model, key_block):
    mod = mod_ref[...]
    shift = mod[..., 0:d_model]
    scale = mod[..., d_model:2 * d_model]
    parts = [slice(c * key_block, (c + 1) * key_block) for c in range(x_ref.shape[0] // key_block)]
    hs = [(_layer_norm(x_ref[part, :]) * (1.0 + scale) + shift).astype(BF16) for part in parts]
    uqs = [_dot(h, w_uq_ref[...]) for h in hs]
    kv_ts = [_dot_nt(w_kv_t_ref[...], h) for h in hs]
    sb = w_kv_t_ref.shape[0] // 2
    for c, (part, uq, kv_t) in enumerate(zip(parts, uqs, kv_ts)):
        u_ref[part, :] = uq[:, :POOL_WIDTH]
        qb_ref[part, :] = (uq[:, POOL_WIDTH:] * (SB_SCALE * LOG2_E)).astype(BF16)
        for f32_ref, bf_ref, val in ((k_ref, kb_ref, kv_t[:sb]), (v_ref, vb_ref, kv_t[sb:])):
            f32_ref[:, part] = val
            bf_ref[c] = val.astype(BF16)


def _inproj_prompt(x, mod, w_uq_bf, w_kv_t_bf, layer):
    seq, d_model = x.shape
    n_uq = w_uq_bf.shape[-1]
    sb = w_kv_t_bf.shape[1] // 2
    tm = ROW_TILE
    key_block = ATTN_TILE
    row_spec = lambda w: pl.BlockSpec((tm, w), lambda i: (i, 0))
    t_spec = pl.BlockSpec((sb, tm), lambda i: (0, i))
    blocks_spec = pl.BlockSpec((tm // key_block, sb, key_block), lambda i: (i, 0, 0))
    return pl.pallas_call(
        functools.partial(_inproj_prompt_kernel, d_model=d_model, key_block=key_block),
        grid=(seq // tm,),
        in_specs=[row_spec(d_model), pl.BlockSpec(mod.shape, lambda i: (0, 0)),
                  pl.BlockSpec((None, d_model, n_uq), lambda i: (layer, 0, 0)),
                  pl.BlockSpec((None, 2 * sb, d_model), lambda i: (layer, 0, 0))],
        out_specs=[row_spec(POOL_WIDTH), t_spec, t_spec, row_spec(sb), blocks_spec, blocks_spec],
        out_shape=[jax.ShapeDtypeStruct((seq, POOL_WIDTH), F32),
                   jax.ShapeDtypeStruct((sb, seq), F32),
                   jax.ShapeDtypeStruct((sb, seq), F32),
                   jax.ShapeDtypeStruct((seq, sb), BF16),
                   jax.ShapeDtypeStruct((seq // key_block, sb, key_block), BF16),
                   jax.ShapeDtypeStruct((seq // key_block, sb, key_block), BF16)],
        compiler_params=pltpu.CompilerParams(
            dimension_semantics=("arbitrary",), vmem_limit_bytes=V7X_VMEM_LIMIT_BYTES),
        name="inproj_prompt",
    )(x, mod, w_uq_bf, w_kv_t_bf)


def _sb_bits(z2s, suffix):
    log_betas, later, sums = [], [], []
    for z2 in z2s:
        bits = _softplus2(z2)
        log_betas.append(z2 - bits)
        later.append(_dot(bits.astype(BF16), suffix))
        sums.append(later[-1][:, :1] + bits[:, :1])
    return log_betas, later, sums


def _sb_weighted(log_betas, later, carries, v_bfs, v_transposed=False):
    pv_dot = _dot_nt if v_transposed else _dot
    return [pv_dot(jnp.exp2(lb - (l + c)).astype(BF16), v) for lb, l, c, v in zip(log_betas, later, carries, v_bfs)]


def _masked(z2s, mask):
    return z2s if mask is None else [jnp.where(mask, z2, MASKED_LOGIT) for z2 in z2s]


def _sb_blocks(z2s, v_bfs, suffix, carries, mask, v_transposed=False):
    log_betas, later, sums = _sb_bits(_masked(z2s, mask), suffix)
    return _sb_weighted(log_betas, later, carries, v_bfs, v_transposed), sums


def _sb_chain(z2s, v_bfs, suffix, carry, mask, v_transposed=False):
    log_betas, later, sums = _sb_bits(_masked(z2s, mask), suffix)
    carries = [carry]
    for block_sum in sums:
        carries.append(carries[-1] + block_sum)
    pvs = _sb_weighted(log_betas, later, carries[:-1], v_bfs, v_transposed)
    total = pvs[0]
    for pv in pvs[1:]:
        total = total + pv
    return total, carries[-1]


def _attn_prompt_kernel(q_ref, k_hbm, v_hbm, o_ref, k_ref, v_ref, sem, carry_ref, acc_ref, *, tile, heads):
    i = pl.program_id(0)

    def block_copies(j):
        return (pltpu.make_async_copy(k_hbm.at[j], k_ref.at[j], sem.at[0, j]),
                pltpu.make_async_copy(v_hbm.at[j], v_ref.at[j], sem.at[1, j]))

    @pl.when(i == 0)
    def _():
        def start(j, _):
            for cp in block_copies(j):
                cp.start()
            return 0
        lax.fori_loop(0, pl.num_programs(0), start, 0)

    for cp in block_copies(i):
        cp.wait()
    lane = lax.broadcasted_iota(jnp.int32, (tile, HEAD_PAIR_WIDTH), 1)
    low = lane < SB_HEAD_DIM
    q_heads = []
    for h in range(heads):
        qp = q_ref[:, (h // 2) * HEAD_PAIR_WIDTH:(h // 2 + 1) * HEAD_PAIR_WIDTH]
        q_heads.append(jnp.where(low if h % 2 == 0 else ~low, qp, jnp.zeros_like(qp)))
    suffix = _suffix_ones(tile)
    row = lax.broadcasted_iota(jnp.int32, (tile, tile), 0)
    col = lax.broadcasted_iota(jnp.int32, (tile, tile), 1)
    causal = col < row

    def block(j, mask, first):
        pair_rows = lambda h: slice((h // 2) * HEAD_PAIR_WIDTH, (h // 2 + 1) * HEAD_PAIR_WIDTH)
        z2s = [_dot(q_heads[h], k_ref[j, pair_rows(h), :]) for h in range(heads)]
        v_blks = [v_ref[j, pair_rows(h), :] for h in range(heads)]
        carries = [jnp.zeros((tile, 1), F32) if first else carry_ref[h] for h in range(heads)]
        pvs, blk_bits = _sb_blocks(z2s, v_blks, suffix, carries, mask, v_transposed=True)
        least = None
        for h in range(heads):
            carry = carries[h] + blk_bits[h]
            carry_ref[h] = carry
            acc_ref[h] = pvs[h] if first else acc_ref[h] + pvs[h]
            least = carry if least is None else jnp.minimum(least, carry)
        return jnp.min(least)

    least = block(i, causal, True)

    def more(state):
        j, least = state
        return (j >= 0) & (least < F32_UNDERFLOW_BITS)

    lax.while_loop(more, lambda state: (state[0] - 1, block(state[0], None, False)), (i - 1, least))
    for p in range(heads // 2):
        out = jnp.where(low, acc_ref[2 * p], acc_ref[2 * p + 1])
        o_ref[:, p * HEAD_PAIR_WIDTH:(p + 1) * HEAD_PAIR_WIDTH] = out.astype(o_ref.dtype)


def _attn_prompt(q_bf, k_bf, v_bf):
    seq, width = q_bf.shape
    tile = ATTN_TILE
    heads = width // SB_HEAD_DIM
    n_blocks = seq // tile
    assert k_bf.shape == (n_blocks, width, tile)
    return pl.pallas_call(
        functools.partial(_attn_prompt_kernel, tile=tile, heads=heads),
        grid=(n_blocks,),
        in_specs=[pl.BlockSpec((tile, width), lambda i: (i, 0)),
                  pl.BlockSpec(memory_space=pl.ANY), pl.BlockSpec(memory_space=pl.ANY)],
        out_specs=pl.BlockSpec((tile, width), lambda i: (i, 0)),
        out_shape=jax.ShapeDtypeStruct((seq, width), BF16),
        scratch_shapes=[pltpu.VMEM(k_bf.shape, BF16), pltpu.VMEM(v_bf.shape, BF16),
                        pltpu.SemaphoreType.DMA((2, n_blocks)),
                        pltpu.VMEM((heads, tile, 1), F32),
                        pltpu.VMEM((heads, tile, HEAD_PAIR_WIDTH), F32)],
        compiler_params=pltpu.CompilerParams(
            dimension_semantics=("arbitrary",), vmem_limit_bytes=V7X_VMEM_LIMIT_BYTES),
        name="attn_prompt",
    )(q_bf, k_bf, v_bf)


def _attn_sample_kernel(q_ref, kn_ref, vn_ref, ck_hbm, cv_hbm, o_ref, kbuf, vbuf, sem, qs_ref, carry_ref, acc_ref,
                        *, layer, heads, past):
    n_batch, t_new, width = q_ref.shape
    q_rows = heads * t_new
    chunk = SAMPLE_CHUNK_TOKENS
    n_chunks = past // chunk
    lane = lax.broadcasted_iota(jnp.int32, (t_new, width), 1)
    in_head = [(lane >= h * SB_HEAD_DIM) & (lane < (h + 1) * SB_HEAD_DIM) for h in range(heads)]
    row = lax.broadcasted_iota(jnp.int32, (q_rows, t_new), 0)
    col = lax.broadcasted_iota(jnp.int32, (q_rows, t_new), 1)
    causal_new = col < (row % t_new)
    suffix_new = _suffix_ones(t_new)
    suffix = _suffix_ones(chunk)

    def chunk_copies(b, c, slot):
        tokens = pl.ds(pl.multiple_of(past - (c + 1) * chunk, chunk), chunk)
        return (pltpu.make_async_copy(ck_hbm.at[layer, b, :, :, tokens], kbuf.at[slot], sem.at[0, slot]),
                pltpu.make_async_copy(cv_hbm.at[layer, b, :, :, tokens], vbuf.at[slot], sem.at[1, slot]))

    def add_chunk(slot):
        k_t = kbuf[slot].reshape(width, chunk).astype(BF16)
        v_t = vbuf[slot].reshape(width, chunk).astype(BF16)
        pv, carry = _sb_chain([_dot(qs_ref[...], k_t)], [v_t], suffix, carry_ref[...], None, v_transposed=True)
        acc_ref[...] += pv
        carry_ref[...] = carry
        return jnp.min(carry)

    def one_batch(b, _):
        @pl.when(b + 1 < n_batch)
        def _():
            for cp in chunk_copies(b + 1, 0, (b + 1) % 2):
                cp.start()

        q = q_ref[b]
        for h in range(heads):
            qs_ref[h * t_new:(h + 1) * t_new, :] = jnp.where(in_head[h], q, jnp.zeros_like(q))
        pv, carry = _sb_chain([_dot_nt(qs_ref[...], kn_ref[b])], [vn_ref[b]], suffix_new,
                              jnp.zeros((q_rows, 1), F32), causal_new)
        acc_ref[...] = pv
        carry_ref[...] = carry
        for cp in chunk_copies(b, 0, b % 2):
            cp.wait()
        least = add_chunk(b % 2)

        def more(state):
            c, least = state
            return (c < n_chunks) & (least < F32_UNDERFLOW_BITS)

        def older_chunk(state):
            c, _ = state
            copies = chunk_copies(b, c, 2)
            for cp in copies:
                cp.start()
            for cp in copies:
                cp.wait()
            return c + 1, add_chunk(2)

        lax.while_loop(more, older_chunk, (1, least))
        out = jnp.zeros((t_new, width), F32)
        for h in range(heads):
            out = jnp.where(in_head[h], acc_ref[h * t_new:(h + 1) * t_new, :], out)
        o_ref[b] = out.astype(o_ref.dtype)
        return 0

    for cp in chunk_copies(0, 0, 0):
        cp.start()
    lax.fori_loop(0, n_batch, one_batch, 0)


def _attn_sample(q_bf, k_bf, v_bf, cache_k_t, cache_v_t, layer):
    n_batch, t_new, width = q_bf.shape
    heads, head_dim, past = cache_k_t.shape[2:]
    q_rows = heads * t_new
    full = lambda a: pl.BlockSpec(a.shape, lambda i: (0,) * a.ndim)
    chunk_buffers = pltpu.VMEM((3, heads, head_dim, SAMPLE_CHUNK_TOKENS), F32)
    return pl.pallas_call(
        functools.partial(_attn_sample_kernel, layer=layer, heads=heads, past=past),
        grid=(1,),
        in_specs=[full(q_bf), full(k_bf), full(v_bf),
                  pl.BlockSpec(memory_space=pl.ANY), pl.BlockSpec(memory_space=pl.ANY)],
        out_specs=full(q_bf),
        out_shape=jax.ShapeDtypeStruct(q_bf.shape, BF16),
        scratch_shapes=[chunk_buffers, chunk_buffers, pltpu.SemaphoreType.DMA((2, 3)),
                        pltpu.VMEM((q_rows, width), BF16),
                        pltpu.VMEM((q_rows, 1), F32),
                        pltpu.VMEM((q_rows, width), F32)],
        compiler_params=pltpu.CompilerParams(
            dimension_semantics=("arbitrary",), vmem_limit_bytes=V7X_VMEM_LIMIT_BYTES),
        name="attn_sample",
    )(q_bf, k_bf, v_bf, cache_k_t, cache_v_t)


def _trailing_window_sums(u_ext, axis):
    n = u_ext.shape[axis]
    t = n - HALO_ROWS
    sl = lambda a, lo, hi: lax.slice_in_dim(a, lo, hi, axis=axis)
    cols = lambda a, g: a[..., g * POOL_GROUP_WIDTH:]
    s2 = sl(u_ext, 1, n) + sl(u_ext, 0, n - 1)
    s2b = cols(s2, 1)
    s4 = sl(s2b, 2, n - 1) + sl(s2b, 0, n - 3)
    s4b = s4[..., POOL_GROUP_WIDTH:]
    s8 = sl(s4b, 4, n - 3) + sl(s4b, 0, n - 7)
    s8b = s8[..., POOL_GROUP_WIDTH:]
    s16 = sl(s8b, 8, n - 7) + sl(s8b, 0, n - 15)
    first = HALO_ROWS
    return (sl(s2, first - 1, first - 1 + t)[..., :POOL_GROUP_WIDTH],
            sl(s4, first - 3, first - 3 + t)[..., :POOL_GROUP_WIDTH],
            sl(s8, first - 7, first - 7 + t)[..., :POOL_GROUP_WIDTH],
            sl(s16, first - 15, first - 15 + t))


def _post_kernel(x_ref, u_ref, halo_ref, s_ref, mod_ref, wpool_ref, pscale_ref, wo_ref, g1_ref, b1_ref,
                 wgu_ref, wdown_ref, g2_ref, b2_ref, *refs, d_model, d_ff, alpha, ff_chunks, has_state, n_parts,
                 n_stack):
    kv_refs, (o_ref, *stacked_refs) = refs[:2 * n_stack], refs[2 * n_stack:]
    for l in range(n_stack):
        for src, dst in zip(kv_refs[2 * l:2 * l + 2], stacked_refs):
            dst[l] = src[...]
    mod = mod_ref[...]
    gate1 = mod[..., 2 * d_model:3 * d_model]
    shift2 = mod[..., 3 * d_model:4 * d_model]
    scale2 = mod[..., 4 * d_model:5 * d_model]
    gate2 = mod[..., 5 * d_model:6 * d_model]
    row_axis = x_ref.ndim - 2
    rows = x_ref.shape[row_axis] // n_parts
    assert has_state <= (n_parts == 1) and rows >= HALO_ROWS
    parts = [slice(p * rows, (p + 1) * rows) for p in range(n_parts)]
    in_part = lambda ref, part: ref[...] if has_state else ref[part, :]

    def pooled_mix_inputs(p):
        u = in_part(u_ref, parts[p])
        if has_state:
            halo = halo_ref[...]
            inv_counts = [1.0 / w for w in POOL_WINDOWS]
        else:
            if p == 0:
                halo = halo_ref[...]
                halo = jnp.where(pl.program_id(0) == 0, jnp.zeros_like(halo), halo)
            else:
                halo = u_ref[p * rows - HALO_ROWS:p * rows, :]
            t = (pl.program_id(0) * n_parts + p) * rows + lax.broadcasted_iota(jnp.int32, (rows, 1), 0)
            avail = (t + 1).astype(F32)
            inv_counts = [1.0 / jnp.minimum(avail, float(w)) for w in POOL_WINDOWS]
        sums = _trailing_window_sums(jnp.concatenate([halo, u], axis=row_axis), row_axis)
        pool_out = []
        for g, (win_sum, inv) in enumerate(zip(sums, inv_counts)):
            cur = u[..., g * POOL_GROUP_WIDTH:(g + 1) * POOL_GROUP_WIDTH]
            pooled = (win_sum * inv - cur).reshape(-1, POOL_GROUP_WIDTH)
            pool_out.append(_dot(pooled.astype(BF16), wpool_ref[g]))
        pool_out = jnp.concatenate(pool_out, axis=-1) * pscale_ref[...]
        sb_out = in_part(s_ref, parts[p])
        return pool_out.astype(BF16), sb_out.reshape(-1, sb_out.shape[-1]).astype(BF16)

    every = range(n_parts)
    xs = [in_part(x_ref, parts[p]) for p in every]
    mix_in = [pooled_mix_inputs(p) for p in every]
    mix = [_dot(pool_bf, wo_ref[0:POOL_WIDTH, :]) + _dot(sb_bf, wo_ref[POOL_WIDTH:, :]) for pool_bf, sb_bf in mix_in]
    x1 = [_layer_norm(alpha * xs[p] + gate1 * mix[p].reshape(xs[p].shape)) * g1_ref[...] + b1_ref[...] for p in every]
    h = [(_layer_norm(x1[p]) * (1.0 + scale2) + shift2).astype(BF16).reshape(-1, d_model) for p in every]
    ff = [None] * n_parts
    for lo, hi in ff_chunks:
        gate = [_dot(h[p], wgu_ref[:, lo:hi]) for p in every]
        up = [_dot(h[p], wgu_ref[:, d_ff + lo:d_ff + hi]) for p in every]
        act = [(_silu(gate[p]) * up[p]).astype(BF16) for p in every]
        part = [_dot(act[p], wdown_ref[lo:hi, :]) for p in every]
        ff = [part[p] if ff[p] is None else ff[p] + part[p] for p in every]
    for p in every:
        out = _layer_norm(alpha * x1[p] + gate2 * ff[p].reshape(xs[p].shape)) * g2_ref[...] + b2_ref[...]
        if has_state:
            o_ref[...] = out
        else:
            o_ref[parts[p], :] = out


def _ff_chunks(d_ff):
    mxu_cols = 256
    tiles = d_ff // mxu_cols
    assert tiles * mxu_cols == d_ff
    split = (tiles + 1) // 2 * mxu_cols
    return ((0, split), (split, d_ff))


def _post(x, u, halo_src, s, mod, layer, w_pool_bf, pool_scale, w_o_bf, ln1_g, ln1_b, w_gu_bf, w_down_bf,
          ln2_g, ln2_b, alpha, kv_layers=()):
    d_model = x.shape[-1]
    d_ff = w_down_bf.shape[1]
    depth = w_o_bf.shape[0]
    has_state = x.ndim == 3
    if has_state:
        grid = (1,)
        full3 = lambda a: pl.BlockSpec(a.shape, lambda i: (0, 0, 0))
        x_spec, u_spec, halo_spec, s_spec, mod_spec, out_spec = full3(x), full3(u), full3(halo_src), full3(s), full3(mod), full3(x)
    else:
        rows = x.shape[0]
        tm = ROW_TILE
        grid = (rows // tm,)
        halo_blocks = tm // HALO_ROWS
        x_spec = pl.BlockSpec((tm, d_model), lambda i: (i, 0))
        u_spec = pl.BlockSpec((tm, POOL_WIDTH), lambda i: (i, 0))
        halo_spec = pl.BlockSpec((HALO_ROWS, POOL_WIDTH), lambda i: (jnp.maximum(i * halo_blocks - 1, 0), 0))
        s_spec = pl.BlockSpec((tm, s.shape[-1]), lambda i: (i, 0))
        mod_spec = pl.BlockSpec(mod.shape, lambda i: (0, 0))
        out_spec = x_spec
    const = pl.Buffered(1)

    def layer_spec(a):
        nd = a.ndim - 1
        return pl.BlockSpec((None,) + a.shape[1:], lambda i: (layer,) + (0,) * nd, pipeline_mode=const)

    vec = lambda a: a.reshape(depth, 1, a.shape[-1])
    kv_flat = [a for pair in kv_layers for a in pair]
    out_specs, out_shape = [out_spec], [jax.ShapeDtypeStruct(x.shape, F32)]
    if kv_flat:
        sb = kv_flat[0].shape[0]
        kv_specs = [pl.BlockSpec((sb, tm), lambda i: (0, i))] * len(kv_flat)
        out_specs += [pl.BlockSpec((len(kv_layers), sb, tm), lambda i: (0, 0, i))] * 2
        out_shape += [jax.ShapeDtypeStruct((len(kv_layers),) + kv_flat[0].shape, F32)] * 2
    else:
        kv_specs = []
    return pl.pallas_call(
        functools.partial(_post_kernel, d_model=d_model, d_ff=d_ff, alpha=alpha, ff_chunks=_ff_chunks(d_ff),
                          has_state=has_state, n_parts=1 if has_state else POST_ROW_PARTS, n_stack=len(kv_layers)),
        grid=grid,
        in_specs=[x_spec, u_spec, halo_spec, s_spec, mod_spec,
                  layer_spec(w_pool_bf), layer_spec(vec(pool_scale)), layer_spec(w_o_bf),
                  layer_spec(vec(ln1_g)), layer_spec(vec(ln1_b)),
                  layer_spec(w_gu_bf), layer_spec(w_down_bf),
                  layer_spec(vec(ln2_g)), layer_spec(vec(ln2_b))] + kv_specs,
        out_specs=out_specs,
        out_shape=out_shape,
        compiler_params=pltpu.CompilerParams(
            dimension_semantics=("arbitrary",), vmem_limit_bytes=V7X_VMEM_LIMIT_BYTES),
        name="post",
    )(x, u, halo_src, s, mod, w_pool_bf, vec(pool_scale), w_o_bf, vec(ln1_g), vec(ln1_b),
      w_gu_bf, w_down_bf, vec(ln2_g), vec(ln2_b), *kv_flat)


def kernel(x_prompt, x_sample, cache_k, cache_v, state_pool, c_prompt, c_sample, w_ada, b_ada, w_in, w_pool,
           pool_scale, w_o, ln1_g, ln1_b, w_gu, w_down, ln2_g, ln2_b):
    depth, d_model, _ = w_ada.shape
    batch, seq, _ = x_prompt.shape
    dec_batch, dec_seq, _ = x_sample.shape
    past = cache_k.shape[2]
    heads, head_dim = cache_k.shape[3], cache_k.shape[4]
    assert batch == 1 and head_dim == SB_HEAD_DIM and state_pool.shape[2] == POOL_HIST
    assert seq % ROW_TILE == 0 and past % SAMPLE_CHUNK_TOKENS == 0 and dec_seq >= POOL_HIST
    alpha = (2 * depth) ** 0.25

    n_cond = batch + dec_batch
    c_all = jnp.concatenate([c_prompt, c_sample], axis=0)
    c_all = jnp.pad(c_all, ((0, -n_cond % 16), (0, 0)))
    mod = _adaln(c_all, w_ada, b_ada)

    w_in_bf, w_pool_bf, w_o_bf = w_in.astype(BF16), w_pool.astype(BF16), w_o.astype(BF16)
    n_uq = w_in.shape[-1] - 2 * heads * head_dim
    w_uq_bf = w_in_bf[:, :, :n_uq]
    w_kv_t_bf = w_in_bf[:, :, n_uq:].transpose(0, 2, 1)
    w_gu_bf, w_down_bf = w_gu.astype(BF16), w_down.astype(BF16)
    cache_k_t = cache_k.transpose(0, 1, 3, 4, 2)
    cache_v_t = cache_v.transpose(0, 1, 3, 4, 2)
    hist = jnp.pad(state_pool, ((0, 0), (0, 0), (HALO_ROWS - POOL_HIST, 0), (0, 0)))

    xp = x_prompt.reshape(seq, d_model)
    xs = x_sample
    kv_p, pool_p, k_s, v_s, pool_s = [], [], [], [], []
    for l in range(depth):
        post_w = (l, w_pool_bf, pool_scale, w_o_bf, ln1_g, ln1_b, w_gu_bf, w_down_bf, ln2_g, ln2_b, alpha)

        mod_p = mod[l, 0:batch]
        u, k, v, qb, kb, vb = _inproj_prompt(xp, mod_p, w_uq_bf, w_kv_t_bf, l)
        kv_p.append((k, v))
        s = _attn_prompt(qb, kb, vb)
        if l + 1 < depth:
            xp, = _post(xp, u, u, s, mod_p, *post_w)
        else:
            xp, k_p, v_p = _post(xp, u, u, s, mod_p, *post_w, kv_layers=kv_p)
        pool_p.append(u[seq - POOL_HIST:].reshape(batch, POOL_HIST, POOL_WIDTH))

        mod_s = mod[l, batch:n_cond].reshape(dec_batch, 1, -1)
        u, k, v, qb, kb, vb = _inproj(xs, mod_s, w_in_bf, l)
        to3 = lambda a: a.reshape(dec_batch, dec_seq, a.shape[-1])
        s = _attn_sample(to3(qb), to3(kb), to3(vb), cache_k_t, cache_v_t, l)
        u3 = to3(u)
        xs, = _post(xs, u3, hist[l], s, mod_s, *post_w)
        k_s.append(k.reshape(dec_batch, dec_seq, heads, head_dim))
        v_s.append(v.reshape(dec_batch, dec_seq, heads, head_dim))
        pool_s.append(u3[:, dec_seq - POOL_HIST:])

    by_token = lambda a: a.reshape(depth, batch, heads, head_dim, seq).transpose(0, 1, 4, 2, 3)
    return (xp.reshape(batch, seq, d_model), xs, by_token(k_p), by_token(v_p), jnp.stack(pool_p),
            jnp.stack(k_s), jnp.stack(v_s), jnp.stack(pool_s))
```

```python
import functools
import math

import jax
import jax.numpy as jnp
from jax import lax
from jax.experimental import pallas as pl
from jax.experimental.pallas import tpu as pltpu

F32 = jnp.float32
BF16 = jnp.bfloat16

POOL_WINDOWS = (2, 4, 8, 16)
POOL_GROUP_WIDTH = 128
POOL_WIDTH = POOL_GROUP_WIDTH * len(POOL_WINDOWS)
POOL_HIST = max(POOL_WINDOWS) - 1
HALO_ROWS = 16
SB_HEAD_DIM = 64
SB_SCALE = 1.0 / math.sqrt(SB_HEAD_DIM)
LOG2_E = 1.0 / math.log(2.0)
HEAD_PAIR_WIDTH = 2 * SB_HEAD_DIM
LN_EPS = 1e-5
F32_UNDERFLOW_BITS = 152.0
MASKED_LOGIT = -1e4

V7X_VMEM_LIMIT_BYTES = 56 * 1024 * 1024

ROW_TILE = 512
INPROJ_ROW_TILE = 1024
POST_ROW_PARTS = 2
ATTN_TILE = 256
SAMPLE_CHUNK_TOKENS = 256
ADA_COL_TILE = 3072


def _dot(a, b):
    return jnp.dot(a, b, preferred_element_type=F32)


def _dot_nt(a, b):
    return lax.dot_general(a, b, (((1,), (1,)), ((), ())), preferred_element_type=F32)


def _split_bf16(x):
    hi = x.astype(BF16)
    lo = (x - hi.astype(F32)).astype(BF16)
    return hi, lo


def _layer_norm(x):
    mu = jnp.mean(x, axis=-1, keepdims=True)
    xc = x - mu
    var = jnp.mean(xc * xc, axis=-1, keepdims=True)
    return xc * lax.rsqrt(var + LN_EPS)


def _silu(x):
    return x / (1.0 + jnp.exp(-x))


def _softplus2(z2):
    return jnp.maximum(z2, 0.0) + LOG2_E * jnp.log(1.0 + jnp.exp2(-jnp.abs(z2)))


def _suffix_ones(n):
    r = lax.broadcasted_iota(jnp.int32, (n, n), 0)
    c = lax.broadcasted_iota(jnp.int32, (n, n), 1)
    return jnp.where(r > c, 1.0, 0.0).astype(BF16)


def _adaln_kernel(c_ref, w_ref, b_ref, o_ref):
    cond = _silu(c_ref[...])
    c_hi, c_lo = _split_bf16(cond)
    w = w_ref[...].astype(BF16)
    o_ref[...] = _dot(c_hi, w) + _dot(c_lo, w) + b_ref[...]


def _adaln(c_all, w_ada, b_ada):
    depth, d_model, n_out = w_ada.shape
    rows = c_all.shape[0]
    tn = ADA_COL_TILE
    return pl.pallas_call(
        _adaln_kernel,
        grid=(depth, n_out // tn),
        in_specs=[
            pl.BlockSpec((rows, d_model), lambda l, j: (0, 0)),
            pl.BlockSpec((None, d_model, tn), lambda l, j: (l, 0, j)),
            pl.BlockSpec((None, 1, tn), lambda l, j: (l, 0, j)),
        ],
        out_specs=pl.BlockSpec((None, rows, tn), lambda l, j: (l, 0, j)),
        out_shape=jax.ShapeDtypeStruct((depth, rows, n_out), F32),
        compiler_params=pltpu.CompilerParams(
            dimension_semantics=("arbitrary", "arbitrary"), vmem_limit_bytes=V7X_VMEM_LIMIT_BYTES),
        name="adaln",
    )(c_all, w_ada, b_ada.reshape(depth, 1, n_out))


def _inproj_kernel(x_ref, mod_ref, w_ref, u_ref, k_ref, v_ref, qb_ref, kb_ref, vb_ref, *, d_model):
    x = x_ref[...]
    mod = mod_ref[...]
    shift = mod[..., 0:d_model]
    scale = mod[..., d_model:2 * d_model]
    h = (_layer_norm(x) * (1.0 + scale) + shift).astype(BF16)
    h = h.reshape(-1, d_model)
    proj = _dot(h, w_ref[...])
    sb = (proj.shape[-1] - POOL_WIDTH) // 3
    o = POOL_WIDTH
    u = proj[:, :o]
    q = proj[:, o:o + sb]
    k = proj[:, o + sb:o + 2 * sb]
    v = proj[:, o + 2 * sb:o + 3 * sb]
    u_ref[...] = u
    k_ref[...] = k
    v_ref[...] = v
    qb_ref[...] = (q * (SB_SCALE * LOG2_E)).astype(BF16)
    kb_ref[...] = k.astype(BF16)
    vb_ref[...] = v.astype(BF16)


def _inproj(x, mod, w_in_bf, layer):
    d_model = x.shape[-1]
    n_out = w_in_bf.shape[-1]
    sb = (n_out - POOL_WIDTH) // 3
    rows = x.shape[0] * x.shape[1]
    row_spec = lambda w: pl.BlockSpec((rows, w), lambda i: (0, 0))
    return pl.pallas_call(
        functools.partial(_inproj_kernel, d_model=d_model),
        grid=(1,),
        in_specs=[pl.BlockSpec(x.shape, lambda i: (0, 0, 0)), pl.BlockSpec(mod.shape, lambda i: (0, 0, 0)),
                  pl.BlockSpec((None, d_model, n_out), lambda i: (layer, 0, 0))],
        out_specs=[row_spec(POOL_WIDTH), row_spec(sb), row_spec(sb), row_spec(sb), row_spec(sb), row_spec(sb)],
        out_shape=[jax.ShapeDtypeStruct((rows, POOL_WIDTH), F32),
                   jax.ShapeDtypeStruct((rows, sb), F32),
                   jax.ShapeDtypeStruct((rows, sb), F32),
                   jax.ShapeDtypeStruct((rows, sb), BF16),
                   jax.ShapeDtypeStruct((rows, sb), BF16),
                   jax.ShapeDtypeStruct((rows, sb), BF16)],
        compiler_params=pltpu.CompilerParams(
            dimension_semantics=("arbitrary",), vmem_limit_bytes=V7X_VMEM_LIMIT_BYTES),
        name="inproj",
    )(x, mod, w_in_bf)


def _inproj_prompt_kernel(x_ref, mod_ref, w_uq_ref, w_kv_t_ref, u_ref, k_ref, v_ref, qb_ref, kb_ref, vb_ref,
                          *, d_model, key_block):
    mod = mod_ref[...]
    shift = mod[..., 0:d_model]
    scale = mod[..., d_model:2 * d_model]
    parts = [slice(c * key_block, (c + 1) * key_block) for c in range(x_ref.shape[0] // key_block)]
    hs = [(_layer_norm(x_ref[part, :]) * (1.0 + scale) + shift).astype(BF16) for part in parts]
    uqs = [_dot(h, w_uq_ref[...]) for h in hs]
    kv_ts = [_dot_nt(w_kv_t_ref[...], h) for h in hs]
    sb = w_kv_t_ref.shape[0] // 2
    for c, (part, uq, kv_t) in enumerate(zip(parts, uqs, kv_ts)):
        u_ref[part, :] = uq[:, :POOL_WIDTH]
        qb_ref[part, :] = (uq[:, POOL_WIDTH:] * (SB_SCALE * LOG2_E)).astype(BF16)
        for f32_ref, bf_ref, val in ((k_ref, kb_ref, kv_t[:sb]), (v_ref, vb_ref, kv_t[sb:])):
            f32_ref[:, part] = val
            bf_ref[c] = val.astype(BF16)


def _inproj_prompt(x, mod, w_uq_bf, w_kv_t_bf, layer):
    seq, d_model = x.shape
    n_uq = w_uq_bf.shape[-1]
    sb = w_kv_t_bf.shape[1] // 2
    tm = INPROJ_ROW_TILE
    key_block = ATTN_TILE
    row_spec = lambda w: pl.BlockSpec((tm, w), lambda i: (i, 0))
    t_spec = pl.BlockSpec((sb, tm), lambda i: (0, i))
    blocks_spec = pl.BlockSpec((tm // key_block, sb, key_block), lambda i: (i, 0, 0))
    return pl.pallas_call(
        functools.partial(_inproj_prompt_kernel, d_model=d_model, key_block=key_block),
        grid=(seq // tm,),
        in_specs=[row_spec(d_model), pl.BlockSpec(mod.shape, lambda i: (0, 0)),
                  pl.BlockSpec((None, d_model, n_uq), lambda i: (layer, 0, 0)),
                  pl.BlockSpec((None, 2 * sb, d_model), lambda i: (layer, 0, 0))],
        out_specs=[row_spec(POOL_WIDTH), t_spec, t_spec, row_spec(sb), blocks_spec, blocks_spec],
        out_shape=[jax.ShapeDtypeStruct((seq, POOL_WIDTH), F32),
                   jax.ShapeDtypeStruct((sb, seq), F32),
                   jax.ShapeDtypeStruct((sb, seq), F32),
                   jax.ShapeDtypeStruct((seq, sb), BF16),
                   jax.ShapeDtypeStruct((seq // key_block, sb, key_block), BF16),
                   jax.ShapeDtypeStruct((seq // key_block, sb, key_block), BF16)],
        compiler_params=pltpu.CompilerParams(
            dimension_semantics=("arbitrary",), vmem_limit_bytes=V7X_VMEM_LIMIT_BYTES),
        name="inproj_prompt",
    )(x, mod, w_uq_bf, w_kv_t_bf)


def _sb_bits(z2s, suffix):
    log_betas, later, sums = [], [], []
    for z2 in z2s:
        bits = _softplus2(z2)
        log_betas.append(z2 - bits)
        later.append(_dot(bits.astype(BF16), suffix))
        sums.append(later[-1][:, :1] + bits[:, :1])
    return log_betas, later, sums


def _sb_weighted(log_betas, later, carries, v_bfs, v_transposed=False):
    pv_dot = _dot_nt if v_transposed else _dot
    return [pv_dot(jnp.exp2(lb - (l + c)).astype(BF16), v) for lb, l, c, v in zip(log_betas, later, carries, v_bfs)]


def _masked(z2s, mask):
    return z2s if mask is None else [jnp.where(mask, z2, MASKED_LOGIT) for z2 in z2s]


def _sb_blocks(z2s, v_bfs, suffix, carries, mask, v_transposed=False):
    log_betas, later, sums = _sb_bits(_masked(z2s, mask), suffix)
    return _sb_weighted(log_betas, later, carries, v_bfs, v_transposed), sums


def _sb_chain(z2s, v_bfs, suffix, carry, mask, v_transposed=False):
    log_betas, later, sums = _sb_bits(_masked(z2s, mask), suffix)
    carries = [carry]
    for block_sum in sums:
        carries.append(carries[-1] + block_sum)
    pvs = _sb_weighted(log_betas, later, carries[:-1], v_bfs, v_transposed)
    total = pvs[0]
    for pv in pvs[1:]:
        total = total + pv
    return total, carries[-1]


def _attn_prompt_kernel(q_ref, k_ref, v_ref, o_ref, carry_ref, acc_ref, *, tile, heads):
    i = pl.program_id(0)
    lane = lax.broadcasted_iota(jnp.int32, (tile, HEAD_PAIR_WIDTH), 1)
    low = lane < SB_HEAD_DIM
    q_heads = []
    for h in range(heads):
        qp = q_ref[:, (h // 2) * HEAD_PAIR_WIDTH:(h // 2 + 1) * HEAD_PAIR_WIDTH]
        q_heads.append(jnp.where(low if h % 2 == 0 else ~low, qp, jnp.zeros_like(qp)))
    suffix = _suffix_ones(tile)
    row = lax.broadcasted_iota(jnp.int32, (tile, tile), 0)
    col = lax.broadcasted_iota(jnp.int32, (tile, tile), 1)
    causal = col < row

    def block(j, mask, first):
        pair_rows = lambda h: slice((h // 2) * HEAD_PAIR_WIDTH, (h // 2 + 1) * HEAD_PAIR_WIDTH)
        z2s = [_dot(q_heads[h], k_ref[j, pair_rows(h), :]) for h in range(heads)]
        v_blks = [v_ref[j, pair_rows(h), :] for h in range(heads)]
        carries = [jnp.zeros((tile, 1), F32) if first else carry_ref[h] for h in range(heads)]
        pvs, blk_bits = _sb_blocks(z2s, v_blks, suffix, carries, mask, v_transposed=True)
        least = None
        for h in range(heads):
            carry = carries[h] + blk_bits[h]
            carry_ref[h] = carry
            acc_ref[h] = pvs[h] if first else acc_ref[h] + pvs[h]
            least = carry if least is None else jnp.minimum(least, carry)
        return jnp.min(least)

    least = block(i, causal, True)

    def more(state):
        j, least = state
        return (j >= 0) & (least < F32_UNDERFLOW_BITS)

    lax.while_loop(more, lambda state: (state[0] - 1, block(state[0], None, False)), (i - 1, least))
    for p in range(heads // 2):
        out = jnp.where(low, acc_ref[2 * p], acc_ref[2 * p + 1])
        o_ref[:, p * HEAD_PAIR_WIDTH:(p + 1) * HEAD_PAIR_WIDTH] = out.astype(o_ref.dtype)


def _attn_prompt(q_bf, k_bf, v_bf):
    seq, width = q_bf.shape
    tile = ATTN_TILE
    heads = width // SB_HEAD_DIM
    resident = lambda: pl.BlockSpec(k_bf.shape, lambda i: (0, 0, 0), pipeline_mode=pl.Buffered(1))
    return pl.pallas_call(
        functools.partial(_attn_prompt_kernel, tile=tile, heads=heads),
        grid=(seq // tile,),
        in_specs=[pl.BlockSpec((tile, width), lambda i: (i, 0)), resident(), resident()],
        out_specs=pl.BlockSpec((tile, width), lambda i: (i, 0)),
        out_shape=jax.ShapeDtypeStruct((seq, width), BF16),
        scratch_shapes=[pltpu.VMEM((heads, tile, 1), F32),
                        pltpu.VMEM((heads, tile, HEAD_PAIR_WIDTH), F32)],
        compiler_params=pltpu.CompilerParams(
            dimension_semantics=("arbitrary",), vmem_limit_bytes=V7X_VMEM_LIMIT_BYTES),
        name="attn_prompt",
    )(q_bf, k_bf, v_bf)


def _attn_sample_kernel(q_ref, kn_ref, vn_ref, ck_hbm, cv_hbm, o_ref, kbuf, vbuf, sem, qs_ref, carry_ref, acc_ref,
                        *, layer, heads, past):
    n_batch, t_new, width = q_ref.shape
    q_rows = heads * t_new
    chunk = SAMPLE_CHUNK_TOKENS
    n_chunks = past // chunk
    lane = lax.broadcasted_iota(jnp.int32, (t_new, width), 1)
    in_head = [(lane >= h * SB_HEAD_DIM) & (lane < (h + 1) * SB_HEAD_DIM) for h in range(heads)]
    row = lax.broadcasted_iota(jnp.int32, (q_rows, t_new), 0)
    col = lax.broadcasted_iota(jnp.int32, (q_rows, t_new), 1)
    causal_new = col < (row % t_new)
    suffix_new = _suffix_ones(t_new)
    suffix = _suffix_ones(chunk)

    def chunk_copies(b, c, slot):
        tokens = pl.ds(pl.multiple_of(past - (c + 1) * chunk, chunk), chunk)
        return (pltpu.make_async_copy(ck_hbm.at[layer, b, :, :, tokens], kbuf.at[slot], sem.at[0, slot]),
                pltpu.make_async_copy(cv_hbm.at[layer, b, :, :, tokens], vbuf.at[slot], sem.at[1, slot]))

    def add_chunk(slot):
        k_t = kbuf[slot].reshape(width, chunk).astype(BF16)
        v_t = vbuf[slot].reshape(width, chunk).astype(BF16)
        pv, carry = _sb_chain([_dot(qs_ref[...], k_t)], [v_t], suffix, carry_ref[...], None, v_transposed=True)
        acc_ref[...] += pv
        carry_ref[...] = carry
        return jnp.min(carry)

    def one_batch(b, _):
        @pl.when(b + 1 < n_batch)
        def _():
            for cp in chunk_copies(b + 1, 0, (b + 1) % 2):
                cp.start()

        q = q_ref[b]
        for h in range(heads):
            qs_ref[h * t_new:(h + 1) * t_new, :] = jnp.where(in_head[h], q, jnp.zeros_like(q))
        for cp in chunk_copies(b, 0, b % 2):
            cp.wait()
        qs = qs_ref[...]
        k_t = kbuf[b % 2].reshape(width, chunk).astype(BF16)
        v_t = vbuf[b % 2].reshape(width, chunk).astype(BF16)
        lb_new, later_new, bits_new = _sb_bits(_masked([_dot_nt(qs, kn_ref[b])], causal_new), suffix_new)
        lb_old, later_old, bits_old = _sb_bits([_dot(qs, k_t)], suffix)
        pv_new, = _sb_weighted(lb_new, later_new, [jnp.zeros((q_rows, 1), F32)], [vn_ref[b]])
        pv_old, = _sb_weighted(lb_old, later_old, bits_new, [v_t], v_transposed=True)
        carry = bits_new[0] + bits_old[0]
        acc_ref[...] = pv_new + pv_old
        carry_ref[...] = carry
        least = jnp.min(carry)

        def more(state):
            c, least = state
            return (c < n_chunks) & (least < F32_UNDERFLOW_BITS)

        def older_chunk(state):
            c, _ = state
            copies = chunk_copies(b, c, 2)
            for cp in copies:
                cp.start()
            for cp in copies:
                cp.wait()
            return c + 1, add_chunk(2)

        lax.while_loop(more, older_chunk, (1, least))
        out = jnp.zeros((t_new, width), F32)
        for h in range(heads):
            out = jnp.where(in_head[h], acc_ref[h * t_new:(h + 1) * t_new, :], out)
        o_ref[b] = out.astype(o_ref.dtype)
        return 0

    for cp in chunk_copies(0, 0, 0):
        cp.start()
    lax.fori_loop(0, n_batch, one_batch, 0)


def _attn_sample(q_bf, k_bf, v_bf, cache_k_t, cache_v_t, layer):
    n_batch, t_new, width = q_bf.shape
    heads, head_dim, past = cache_k_t.shape[2:]
    q_rows = heads * t_new
    full = lambda a: pl.BlockSpec(a.shape, lambda i: (0,) * a.ndim)
    chunk_buffers = pltpu.VMEM((3, heads, head_dim, SAMPLE_CHUNK_TOKENS), F32)
    return pl.pallas_call(
        functools.partial(_attn_sample_kernel, layer=layer, heads=heads, past=past),
        grid=(1,),
        in_specs=[full(q_bf), full(k_bf), full(v_bf),
                  pl.BlockSpec(memory_space=pl.ANY), pl.BlockSpec(memory_space=pl.ANY)],
        out_specs=full(q_bf),
        out_shape=jax.ShapeDtypeStruct(q_bf.shape, BF16),
        scratch_shapes=[chunk_buffers, chunk_buffers, pltpu.SemaphoreType.DMA((2, 3)),
                        pltpu.VMEM((q_rows, width), BF16),
                        pltpu.VMEM((q_rows, 1), F32),
                        pltpu.VMEM((q_rows, width), F32)],
        compiler_params=pltpu.CompilerParams(
            dimension_semantics=("arbitrary",), vmem_limit_bytes=V7X_VMEM_LIMIT_BYTES),
        name="attn_sample",
    )(q_bf, k_bf, v_bf, cache_k_t, cache_v_t)


def _trailing_window_sums(u_ext, axis):
    n = u_ext.shape[axis]
    t = n - HALO_ROWS
    sl = lambda a, lo, hi: lax.slice_in_dim(a, lo, hi, axis=axis)
    cols = lambda a, g: a[..., g * POOL_GROUP_WIDTH:]
    s2 = sl(u_ext, 1, n) + sl(u_ext, 0, n - 1)
    s2b = cols(s2, 1)
    s4 = sl(s2b, 2, n - 1) + sl(s2b, 0, n - 3)
    s4b = s4[..., POOL_GROUP_WIDTH:]
    s8 = sl(s4b, 4, n - 3) + sl(s4b, 0, n - 7)
    s8b = s8[..., POOL_GROUP_WIDTH:]
    s16 = sl(s8b, 8, n - 7) + sl(s8b, 0, n - 15)
    first = HALO_ROWS
    return (sl(s2, first - 1, first - 1 + t)[..., :POOL_GROUP_WIDTH],
            sl(s4, first - 3, first - 3 + t)[..., :POOL_GROUP_WIDTH],
            sl(s8, first - 7, first - 7 + t)[..., :POOL_GROUP_WIDTH],
            sl(s16, first - 15, first - 15 + t))


def _post_kernel(x_ref, u_ref, halo_ref, s_ref, mod_ref, wpool_ref, pscale_ref, wo_ref, g1_ref, b1_ref,
                 wgu_ref, wdown_ref, g2_ref, b2_ref, *refs, d_model, d_ff, alpha, ff_chunks, has_state, n_parts,
                 n_stack):
    kv_refs, (o_ref, *stacked_refs) = refs[:2 * n_stack], refs[2 * n_stack:]
    for l in range(n_stack):
        for src, dst in zip(kv_refs[2 * l:2 * l + 2], stacked_refs):
            dst[l] = src[...]
    mod = mod_ref[...]
    gate1 = mod[..., 2 * d_model:3 * d_model]
    shift2 = mod[..., 3 * d_model:4 * d_model]
    scale2 = mod[..., 4 * d_model:5 * d_model]
    gate2 = mod[..., 5 * d_model:6 * d_model]
    row_axis = x_ref.ndim - 2
    rows = x_ref.shape[row_axis] // n_parts
    assert has_state <= (n_parts == 1) and rows >= HALO_ROWS
    parts = [slice(p * rows, (p + 1) * rows) for p in range(n_parts)]
    in_part = lambda ref, part: ref[...] if has_state else ref[part, :]

    def pooled_mix_inputs(p):
        u = in_part(u_ref, parts[p])
        if has_state:
            halo = halo_ref[...]
            inv_counts = [1.0 / w for w in POOL_WINDOWS]
        else:
            if p == 0:
                halo = halo_ref[...]
                halo = jnp.where(pl.program_id(0) == 0, jnp.zeros_like(halo), halo)
            else:
                halo = u_ref[p * rows - HALO_ROWS:p * rows, :]
            t = (pl.program_id(0) * n_parts + p) * rows + lax.broadcasted_iota(jnp.int32, (rows, 1), 0)
            avail = (t + 1).astype(F32)
            inv_counts = [1.0 / jnp.minimum(avail, float(w)) for w in POOL_WINDOWS]
        sums = _trailing_window_sums(jnp.concatenate([halo, u], axis=row_axis), row_axis)
        pool_out = []
        for g, (win_sum, inv) in enumerate(zip(sums, inv_counts)):
            cur = u[..., g * POOL_GROUP_WIDTH:(g + 1) * POOL_GROUP_WIDTH]
            pooled = (win_sum * inv - cur).reshape(-1, POOL_GROUP_WIDTH)
            pool_out.append(_dot(pooled.astype(BF16), wpool_ref[g]))
        pool_out = jnp.concatenate(pool_out, axis=-1) * pscale_ref[...]
        sb_out = in_part(s_ref, parts[p])
        return pool_out.astype(BF16), sb_out.reshape(-1, sb_out.shape[-1]).astype(BF16)

    every = range(n_parts)
    xs = [in_part(x_ref, parts[p]) for p in every]
    mix_in = [pooled_mix_inputs(p) for p in every]
    mix = [_dot(pool_bf, wo_ref[0:POOL_WIDTH, :]) + _dot(sb_bf, wo_ref[POOL_WIDTH:, :]) for pool_bf, sb_bf in mix_in]
    x1 = [_layer_norm(alpha * xs[p] + gate1 * mix[p].reshape(xs[p].shape)) * g1_ref[...] + b1_ref[...] for p in every]
    h = [(_layer_norm(x1[p]) * (1.0 + scale2) + shift2).astype(BF16).reshape(-1, d_model) for p in every]
    ff = [None] * n_parts
    for lo, hi in ff_chunks:
        gate = [_dot(h[p], wgu_ref[:, lo:hi]) for p in every]
        up = [_dot(h[p], wgu_ref[:, d_ff + lo:d_ff + hi]) for p in every]
        act = [(_silu(gate[p]) * up[p]).astype(BF16) for p in every]
        part = [_dot(act[p], wdown_ref[lo:hi, :]) for p in every]
        ff = [part[p] if ff[p] is None else ff[p] + part[p] for p in every]
    for p in every:
        out = _layer_norm(alpha * x1[p] + gate2 * ff[p].reshape(xs[p].shape)) * g2_ref[...] + b2_ref[...]
        if has_state:
            o_ref[...] = out
        else:
            o_ref[parts[p], :] = out


def _ff_chunks(d_ff):
    mxu_cols = 256
    tiles = d_ff // mxu_cols
    assert tiles * mxu_cols == d_ff
    split = (tiles + 1) // 2 * mxu_cols
    return ((0, split), (split, d_ff))


def _post(x, u, halo_src, s, mod, layer, w_pool_bf, pool_scale, w_o_bf, ln1_g, ln1_b, w_gu_bf, w_down_bf,
          ln2_g, ln2_b, alpha, kv_layers=()):
    d_model = x.shape[-1]
    d_ff = w_down_bf.shape[1]
    depth = w_o_bf.shape[0]
    has_state = x.ndim == 3
    if has_state:
        grid = (1,)
        full3 = lambda a: pl.BlockSpec(a.shape, lambda i: (0, 0, 0))
        x_spec, u_spec, halo_spec, s_spec, mod_spec, out_spec = full3(x), full3(u), full3(halo_src), full3(s), full3(mod), full3(x)
    else:
        rows = x.shape[0]
        tm = ROW_TILE
        grid = (rows // tm,)
        halo_blocks = tm // HALO_ROWS
        x_spec = pl.BlockSpec((tm, d_model), lambda i: (i, 0))
        u_spec = pl.BlockSpec((tm, POOL_WIDTH), lambda i: (i, 0))
        halo_spec = pl.BlockSpec((HALO_ROWS, POOL_WIDTH), lambda i: (jnp.maximum(i * halo_blocks - 1, 0), 0))
        s_spec = pl.BlockSpec((tm, s.shape[-1]), lambda i: (i, 0))
        mod_spec = pl.BlockSpec(mod.shape, lambda i: (0, 0))
        out_spec = x_spec
    const = pl.Buffered(1)

    def layer_spec(a):
        nd = a.ndim - 1
        return pl.BlockSpec((None,) + a.shape[1:], lambda i: (layer,) + (0,) * nd, pipeline_mode=const)

    vec = lambda a: a.reshape(depth, 1, a.shape[-1])
    kv_flat = [a for pair in kv_layers for a in pair]
    out_specs, out_shape = [out_spec], [jax.ShapeDtypeStruct(x.shape, F32)]
    if kv_flat:
        sb = kv_flat[0].shape[0]
        kv_specs = [pl.BlockSpec((sb, tm), lambda i: (0, i))] * len(kv_flat)
        out_specs += [pl.BlockSpec((len(kv_layers), sb, tm), lambda i: (0, 0, i))] * 2
        out_shape += [jax.ShapeDtypeStruct((len(kv_layers),) + kv_flat[0].shape, F32)] * 2
    else:
        kv_specs = []
    return pl.pallas_call(
        functools.partial(_post_kernel, d_model=d_model, d_ff=d_ff, alpha=alpha, ff_chunks=_ff_chunks(d_ff),
                          has_state=has_state, n_parts=1 if has_state else POST_ROW_PARTS, n_stack=len(kv_layers)),
        grid=grid,
        in_specs=[x_spec, u_spec, halo_spec, s_spec, mod_spec,
                  layer_spec(w_pool_bf), layer_spec(vec(pool_scale)), layer_spec(w_o_bf),
                  layer_spec(vec(ln1_g)), layer_spec(vec(ln1_b)),
                  layer_spec(w_gu_bf), layer_spec(w_down_bf),
                  layer_spec(vec(ln2_g)), layer_spec(vec(ln2_b))] + kv_specs,
        out_specs=out_specs,
        out_shape=out_shape,
        compiler_params=pltpu.CompilerParams(
            dimension_semantics=("arbitrary",), vmem_limit_bytes=V7X_VMEM_LIMIT_BYTES),
        name="post",
    )(x, u, halo_src, s, mod, w_pool_bf, vec(pool_scale), w_o_bf, vec(ln1_g), vec(ln1_b),
      w_gu_bf, w_down_bf, vec(ln2_g), vec(ln2_b), *kv_flat)


def kernel(x_prompt, x_sample, cache_k, cache_v, state_pool, c_prompt, c_sample, w_ada, b_ada, w_in, w_pool,
           pool_scale, w_o, ln1_g, ln1_b, w_gu, w_down, ln2_g, ln2_b):
    depth, d_model, _ = w_ada.shape
    batch, seq, _ = x_prompt.shape
    dec_batch, dec_seq, _ = x_sample.shape
    past = cache_k.shape[2]
    heads, head_dim = cache_k.shape[3], cache_k.shape[4]
    assert batch == 1 and head_dim == SB_HEAD_DIM and state_pool.shape[2] == POOL_HIST
    assert seq % ROW_TILE == 0 and seq % INPROJ_ROW_TILE == 0
    assert past % SAMPLE_CHUNK_TOKENS == 0 and dec_seq >= POOL_HIST
    alpha = (2 * depth) ** 0.25

    n_cond = batch + dec_batch
    c_all = jnp.concatenate([c_prompt, c_sample], axis=0)
    c_all = jnp.pad(c_all, ((0, -n_cond % 16), (0, 0)))
    mod = _adaln(c_all, w_ada, b_ada)

    w_in_bf, w_pool_bf, w_o_bf = w_in.astype(BF16), w_pool.astype(BF16), w_o.astype(BF16)
    n_uq = w_in.shape[-1] - 2 * heads * head_dim
    w_uq_bf = w_in_bf[:, :, :n_uq]
    w_kv_t_bf = w_in_bf[:, :, n_uq:].transpose(0, 2, 1)
    w_gu_bf, w_down_bf = w_gu.astype(BF16), w_down.astype(BF16)
    cache_k_t = cache_k.transpose(0, 1, 3, 4, 2)
    cache_v_t = cache_v.transpose(0, 1, 3, 4, 2)
    hist = jnp.pad(state_pool, ((0, 0), (0, 0), (HALO_ROWS - POOL_HIST, 0), (0, 0)))

    xp = x_prompt.reshape(seq, d_model)
    xs = x_sample
    kv_p, pool_p, k_s, v_s, pool_s = [], [], [], [], []
    for l in range(depth):
        post_w = (l, w_pool_bf, pool_scale, w_o_bf, ln1_g, ln1_b, w_gu_bf, w_down_bf, ln2_g, ln2_b, alpha)

        mod_p = mod[l, 0:batch]
        u, k, v, qb, kb, vb = _inproj_prompt(xp, mod_p, w_uq_bf, w_kv_t_bf, l)
        kv_p.append((k, v))
        s = _attn_prompt(qb, kb, vb)
        if l + 1 < depth:
            xp, = _post(xp, u, u, s, mod_p, *post_w)
        else:
            xp, k_p, v_p = _post(xp, u, u, s, mod_p, *post_w, kv_layers=kv_p)
        pool_p.append(u[seq - POOL_HIST:].reshape(batch, POOL_HIST, POOL_WIDTH))

        mod_s = mod[l, batch:n_cond].reshape(dec_batch, 1, -1)
        u, k, v, qb, kb, vb = _inproj(xs, mod_s, w_in_bf, l)
        to3 = lambda a: a.reshape(dec_batch, dec_seq, a.shape[-1])
        s = _attn_sample(to3(qb), to3(kb), to3(vb), cache_k_t, cache_v_t, l)
        u3 = to3(u)
        xs, = _post(xs, u3, hist[l], s, mod_s, *post_w)
        k_s.append(k.reshape(dec_batch, dec_seq, heads, head_dim))
        v_s.append(v.reshape(dec_batch, dec_seq, heads, head_dim))
        pool_s.append(u3[:, dec_seq - POOL_HIST:])

    by_token = lambda a: a.reshape(depth, batch, heads, head_dim, seq).transpose(0, 1, 4, 2, 3)
    return (xp.reshape(batch, seq, d_model), xs, by_token(k_p), by_token(v_p), jnp.stack(pool_p),
            jnp.stack(k_s), jnp.stack(v_s), jnp.stack(pool_s))
```

```python
import functools
import math

import jax
import jax.numpy as jnp
from jax import lax
from jax.experimental import pallas as pl
from jax.experimental.pallas import tpu as pltpu

F32 = jnp.float32
BF16 = jnp.bfloat16

POOL_WINDOWS = (2, 4, 8, 16)
POOL_GROUP_WIDTH = 128
POOL_WIDTH = POOL_GROUP_WIDTH * len(POOL_WINDOWS)
POOL_HIST = max(POOL_WINDOWS) - 1
HALO_ROWS = 16
SB_HEAD_DIM = 64
SB_SCALE = 1.0 / math.sqrt(SB_HEAD_DIM)
LOG2_E = 1.0 / math.log(2.0)
HEAD_PAIR_WIDTH = 2 * SB_HEAD_DIM
LN_EPS = 1e-5
F32_UNDERFLOW_BITS = 152.0
MASKED_LOGIT = -1e4

V7X_VMEM_LIMIT_BYTES = 56 * 1024 * 1024

ROW_TILE = 512
INPROJ_ROW_TILE = 1024
POST_ROW_PARTS = 2
ATTN_TILE = 256
SAMPLE_CHUNK_TOKENS = 256
ADA_COL_TILE = 3072


def _dot(a, b):
    return jnp.dot(a, b, preferred_element_type=F32)


def _dot_nt(a, b):
    return lax.dot_general(a, b, (((1,), (1,)), ((), ())), preferred_element_type=F32)


def _split_bf16(x):
    hi = x.astype(BF16)
    lo = (x - hi.astype(F32)).astype(BF16)
    return hi, lo


def _layer_norm(x):
    mu = jnp.mean(x, axis=-1, keepdims=True)
    xc = x - mu
    var = jnp.mean(xc * xc, axis=-1, keepdims=True)
    return xc * lax.rsqrt(var + LN_EPS)


def _silu(x):
    return x / (1.0 + jnp.exp(-x))


def _softplus2(z2):
    return jnp.maximum(z2, 0.0) + LOG2_E * jnp.log(1.0 + jnp.exp2(-jnp.abs(z2)))


def _suffix_ones(n):
    r = lax.broadcasted_iota(jnp.int32, (n, n), 0)
    c = lax.broadcasted_iota(jnp.int32, (n, n), 1)
    return jnp.where(r > c, 1.0, 0.0).astype(BF16)


def _adaln_kernel(c_ref, w_ref, b_ref, o_ref):
    cond = _silu(c_ref[...])
    c_hi, c_lo = _split_bf16(cond)
    w = w_ref[...].astype(BF16)
    o_ref[...] = _dot(c_hi, w) + _dot(c_lo, w) + b_ref[...]


def _adaln(c_all, w_ada, b_ada):
    depth, d_model, n_out = w_ada.shape
    rows = c_all.shape[0]
    tn = ADA_COL_TILE
    return pl.pallas_call(
        _adaln_kernel,
        grid=(depth, n_out // tn),
        in_specs=[
            pl.BlockSpec((rows, d_model), lambda l, j: (0, 0)),
            pl.BlockSpec((None, d_model, tn), lambda l, j: (l, 0, j)),
            pl.BlockSpec((None, 1, tn), lambda l, j: (l, 0, j)),
        ],
        out_specs=pl.BlockSpec((None, rows, tn), lambda l, j: (l, 0, j)),
        out_shape=jax.ShapeDtypeStruct((depth, rows, n_out), F32),
        compiler_params=pltpu.CompilerParams(
            dimension_semantics=("arbitrary", "arbitrary"), vmem_limit_bytes=V7X_VMEM_LIMIT_BYTES),
        name="adaln",
    )(c_all, w_ada, b_ada.reshape(depth, 1, n_out))


def _inproj_kernel(x_ref, mod_ref, w_ref, u_ref, k_ref, v_ref, qb_ref, kb_ref, vb_ref, *, d_model):
    x = x_ref[...]
    mod = mod_ref[...]
    shift = mod[..., 0:d_model]
    scale = mod[..., d_model:2 * d_model]
    h = (_layer_norm(x) * (1.0 + scale) + shift).astype(BF16)
    h = h.reshape(-1, d_model)
    proj = _dot(h, w_ref[...])
    sb = (proj.shape[-1] - POOL_WIDTH) // 3
    o = POOL_WIDTH
    u = proj[:, :o]
    q = proj[:, o:o + sb]
    k = proj[:, o + sb:o + 2 * sb]
    v = proj[:, o + 2 * sb:o + 3 * sb]
    u_ref[...] = u
    k_ref[...] = k
    v_ref[...] = v
    qb_ref[...] = (q * (SB_SCALE * LOG2_E)).astype(BF16)
    kb_ref[...] = k.astype(BF16)
    vb_ref[...] = v.astype(BF16)


def _inproj(x, mod, w_in_bf, layer):
    d_model = x.shape[-1]
    n_out = w_in_bf.shape[-1]
    sb = (n_out - POOL_WIDTH) // 3
    rows = x.shape[0] * x.shape[1]
    row_spec = lambda w: pl.BlockSpec((rows, w), lambda i: (0, 0))
    return pl.pallas_call(
        functools.partial(_inproj_kernel, d_model=d_model),
        grid=(1,),
        in_specs=[pl.BlockSpec(x.shape, lambda i: (0, 0, 0)), pl.BlockSpec(mod.shape, lambda i: (0, 0, 0)),
                  pl.BlockSpec((None, d_model, n_out), lambda i: (layer, 0, 0))],
        out_specs=[row_spec(POOL_WIDTH), row_spec(sb), row_spec(sb), row_spec(sb), row_spec(sb), row_spec(sb)],
        out_shape=[jax.ShapeDtypeStruct((rows, POOL_WIDTH), F32),
                   jax.ShapeDtypeStruct((rows, sb), F32),
                   jax.ShapeDtypeStruct((rows, sb), F32),
                   jax.ShapeDtypeStruct((rows, sb), BF16),
                   jax.ShapeDtypeStruct((rows, sb), BF16),
                   jax.ShapeDtypeStruct((rows, sb), BF16)],
        compiler_params=pltpu.CompilerParams(
            dimension_semantics=("arbitrary",), vmem_limit_bytes=V7X_VMEM_LIMIT_BYTES),
        name="inproj",
    )(x, mod, w_in_bf)


def _inproj_prompt_kernel(x_ref, mod_ref, w_uq_ref, w_kv_t_ref, u_ref, k_ref, v_ref, qb_ref, kb_ref, vb_ref,
                          *, d_model, key_block):
    mod = mod_ref[...]
    shift = mod[..., 0:d_model]
    scale = mod[..., d_model:2 * d_model]
    parts = [slice(c * key_block, (c + 1) * key_block) for c in range(x_ref.shape[0] // key_block)]
    hs = [(_layer_norm(x_ref[part, :]) * (1.0 + scale) + shift).astype(BF16) for part in parts]
    uqs = [_dot(h, w_uq_ref[...]) for h in hs]
    kv_ts = [_dot_nt(w_kv_t_ref[...], h) for h in hs]
    sb = w_kv_t_ref.shape[0] // 2
    for c, (part, uq, kv_t) in enumerate(zip(parts, uqs, kv_ts)):
        u_ref[part, :] = uq[:, :POOL_WIDTH]
        qb_ref[part, :] = (uq[:, POOL_WIDTH:] * (SB_SCALE * LOG2_E)).astype(BF16)
        for f32_ref, bf_ref, val in ((k_ref, kb_ref, kv_t[:sb]), (v_ref, vb_ref, kv_t[sb:])):
            f32_ref[:, part] = val
            bf_ref[c] = val.astype(BF16)


def _inproj_prompt(x, mod, w_uq_bf, w_kv_t_bf, layer):
    seq, d_model = x.shape
    n_uq = w_uq_bf.shape[-1]
    sb = w_kv_t_bf.shape[1] // 2
    tm = INPROJ_ROW_TILE
    key_block = ATTN_TILE
    row_spec = lambda w: pl.BlockSpec((tm, w), lambda i: (i, 0))
    t_spec = pl.BlockSpec((sb, tm), lambda i: (0, i))
    blocks_spec = pl.BlockSpec((tm // key_block, sb, key_block), lambda i: (i, 0, 0))
    return pl.pallas_call(
        functools.partial(_inproj_prompt_kernel, d_model=d_model, key_block=key_block),
        grid=(seq // tm,),
        in_specs=[row_spec(d_model), pl.BlockSpec(mod.shape, lambda i: (0, 0)),
                  pl.BlockSpec((None, d_model, n_uq), lambda i: (layer, 0, 0)),
                  pl.BlockSpec((None, 2 * sb, d_model), lambda i: (layer, 0, 0))],
        out_specs=[row_spec(POOL_WIDTH), t_spec, t_spec, row_spec(sb), blocks_spec, blocks_spec],
        out_shape=[jax.ShapeDtypeStruct((seq, POOL_WIDTH), F32),
                   jax.ShapeDtypeStruct((sb, seq), F32),
                   jax.ShapeDtypeStruct((sb, seq), F32),
                   jax.ShapeDtypeStruct((seq, sb), BF16),
                   jax.ShapeDtypeStruct((seq // key_block, sb, key_block), BF16),
                   jax.ShapeDtypeStruct((seq // key_block, sb, key_block), BF16)],
        compiler_params=pltpu.CompilerParams(
            dimension_semantics=("arbitrary",), vmem_limit_bytes=V7X_VMEM_LIMIT_BYTES),
        name="inproj_prompt",
    )(x, mod, w_uq_bf, w_kv_t_bf)


def _sb_bits(z2s, suffix):
    log_betas, later, sums = [], [], []
    for z2 in z2s:
        bits = _softplus2(z2)
        log_betas.append(z2 - bits)
        later.append(_dot(bits.astype(BF16), suffix))
        sums.append(later[-1][:, :1] + bits[:, :1])
    return log_betas, later, sums


def _sb_weighted(log_betas, later, carries, v_bfs, v_transposed=False):
    pv_dot = _dot_nt if v_transposed else _dot
    return [pv_dot(jnp.exp2(lb - (l + c)).astype(BF16), v) for lb, l, c, v in zip(log_betas, later, carries, v_bfs)]


def _masked(z2s, mask):
    return z2s if mask is None else [jnp.where(mask, z2, MASKED_LOGIT) for z2 in z2s]


def _sb_blocks(z2s, v_bfs, suffix, carries, mask, v_transposed=False):
    log_betas, later, sums = _sb_bits(_masked(z2s, mask), suffix)
    return _sb_weighted(log_betas, later, carries, v_bfs, v_transposed), sums


def _sb_chain(z2s, v_bfs, suffix, carry, mask, v_transposed=False):
    log_betas, later, sums = _sb_bits(_masked(z2s, mask), suffix)
    carries = [carry]
    for block_sum in sums:
        carries.append(carries[-1] + block_sum)
    pvs = _sb_weighted(log_betas, later, carries[:-1], v_bfs, v_transposed)
    total = pvs[0]
    for pv in pvs[1:]:
        total = total + pv
    return total, carries[-1]


def _attn_prompt_kernel(q_ref, k_ref, v_ref, o_ref, carry_ref, acc_ref, *, tile, heads):
    i = pl.program_id(0)
    lane = lax.broadcasted_iota(jnp.int32, (tile, HEAD_PAIR_WIDTH), 1)
    low = lane < SB_HEAD_DIM
    q_heads = []
    for h in range(heads):
        qp = q_ref[:, (h // 2) * HEAD_PAIR_WIDTH:(h // 2 + 1) * HEAD_PAIR_WIDTH]
        q_heads.append(jnp.where(low if h % 2 == 0 else ~low, qp, jnp.zeros_like(qp)))
    suffix = _suffix_ones(tile)
    row = lax.broadcasted_iota(jnp.int32, (tile, tile), 0)
    col = lax.broadcasted_iota(jnp.int32, (tile, tile), 1)
    causal = col < row

    def older_block(j):
        pair_rows = lambda h: slice((h // 2) * HEAD_PAIR_WIDTH, (h // 2 + 1) * HEAD_PAIR_WIDTH)
        z2s = [_dot(q_heads[h], k_ref[j, pair_rows(h), :]) for h in range(heads)]
        v_blks = [v_ref[j, pair_rows(h), :] for h in range(heads)]
        carries = [carry_ref[h] for h in range(heads)]
        pvs, blk_bits = _sb_blocks(z2s, v_blks, suffix, carries, None, v_transposed=True)
        least = None
        for h in range(heads):
            carry = carries[h] + blk_bits[h]
            carry_ref[h] = carry
            acc_ref[h] = acc_ref[h] + pvs[h]
            least = carry if least is None else jnp.minimum(least, carry)
        return jnp.min(least)

    def diagonal_block():
        half = tile // 2
        pair_rows = lambda h: slice((h // 2) * HEAD_PAIR_WIDTH, (h // 2 + 1) * HEAD_PAIR_WIDTH)
        every = range(heads)
        z_top = [_dot(q_heads[h][:half], k_ref[i, pair_rows(h), :half]) for h in every]
        z_bot = [_dot(q_heads[h][half:], k_ref[i, pair_rows(h), :]) for h in every]
        lb_top, later_top, bits_top = _sb_bits(_masked(z_top, causal[:half, :half]), _suffix_ones(half))
        lb_bot, later_bot, bits_bot = _sb_bits(_masked(z_bot, causal[half:, :]), suffix)
        none_later = [jnp.zeros((half, 1), F32)] * heads
        pv_top = _sb_weighted(lb_top, later_top, none_later, [v_ref[i, pair_rows(h), :half] for h in every], True)
        pv_bot = _sb_weighted(lb_bot, later_bot, none_later, [v_ref[i, pair_rows(h), :] for h in every], True)
        least = None
        for h in every:
            carry_ref[h, :half] = bits_top[h]
            carry_ref[h, half:] = bits_bot[h]
            acc_ref[h, :half] = pv_top[h]
            acc_ref[h, half:] = pv_bot[h]
            low_bits = jnp.minimum(bits_top[h], bits_bot[h])
            least = low_bits if least is None else jnp.minimum(least, low_bits)
        return jnp.min(least)

    least = diagonal_block()

    def more(state):
        j, least = state
        return (j >= 0) & (least < F32_UNDERFLOW_BITS)

    lax.while_loop(more, lambda state: (state[0] - 1, older_block(state[0])), (i - 1, least))
    for p in range(heads // 2):
        out = jnp.where(low, acc_ref[2 * p], acc_ref[2 * p + 1])
        o_ref[:, p * HEAD_PAIR_WIDTH:(p + 1) * HEAD_PAIR_WIDTH] = out.astype(o_ref.dtype)


def _attn_prompt(q_bf, k_bf, v_bf):
    seq, width = q_bf.shape
    tile = ATTN_TILE
    heads = width // SB_HEAD_DIM
    resident = lambda: pl.BlockSpec(k_bf.shape, lambda i: (0, 0, 0), pipeline_mode=pl.Buffered(1))
    return pl.pallas_call(
        functools.partial(_attn_prompt_kernel, tile=tile, heads=heads),
        grid=(seq // tile,),
        in_specs=[pl.BlockSpec((tile, width), lambda i: (i, 0)), resident(), resident()],
        out_specs=pl.BlockSpec((tile, width), lambda i: (i, 0)),
        out_shape=jax.ShapeDtypeStruct((seq, width), BF16),
        scratch_shapes=[pltpu.VMEM((heads, tile, 1), F32),
                        pltpu.VMEM((heads, tile, HEAD_PAIR_WIDTH), F32)],
        compiler_params=pltpu.CompilerParams(
            dimension_semantics=("arbitrary",), vmem_limit_bytes=V7X_VMEM_LIMIT_BYTES),
        name="attn_prompt",
    )(q_bf, k_bf, v_bf)


def _attn_sample_kernel(q_ref, kn_ref, vn_ref, ck_hbm, cv_hbm, o_ref, kbuf, vbuf, sem, qs_ref, carry_ref, acc_ref,
                        *, layer, heads, past):
    n_batch, t_new, width = q_ref.shape
    q_rows = heads * t_new
    chunk = SAMPLE_CHUNK_TOKENS
    n_chunks = past // chunk
    lane = lax.broadcasted_iota(jnp.int32, (t_new, width), 1)
    in_head = [(lane >= h * SB_HEAD_DIM) & (lane < (h + 1) * SB_HEAD_DIM) for h in range(heads)]
    row = lax.broadcasted_iota(jnp.int32, (q_rows, t_new), 0)
    col = lax.broadcasted_iota(jnp.int32, (q_rows, t_new), 1)
    causal_new = col < (row % t_new)
    suffix_new = _suffix_ones(t_new)
    suffix = _suffix_ones(chunk)

    def chunk_copies(b, c, slot):
        tokens = pl.ds(pl.multiple_of(past - (c + 1) * chunk, chunk), chunk)
        return (pltpu.make_async_copy(ck_hbm.at[layer, b, :, :, tokens], kbuf.at[slot], sem.at[0, slot]),
                pltpu.make_async_copy(cv_hbm.at[layer, b, :, :, tokens], vbuf.at[slot], sem.at[1, slot]))

    def add_chunk(slot):
        k_t = kbuf[slot].reshape(width, chunk).astype(BF16)
        v_t = vbuf[slot].reshape(width, chunk).astype(BF16)
        pv, carry = _sb_chain([_dot(qs_ref[...], k_t)], [v_t], suffix, carry_ref[...], None, v_transposed=True)
        acc_ref[...] += pv
        carry_ref[...] = carry
        return jnp.min(carry)

    def one_batch(b, _):
        @pl.when(b + 1 < n_batch)
        def _():
            for cp in chunk_copies(b + 1, 0, (b + 1) % 2):
                cp.start()

        q = q_ref[b]
        for h in range(heads):
            qs_ref[h * t_new:(h + 1) * t_new, :] = jnp.where(in_head[h], q, jnp.zeros_like(q))
        for cp in chunk_copies(b, 0, b % 2):
            cp.wait()
        qs = qs_ref[...]
        k_t = kbuf[b % 2].reshape(width, chunk).astype(BF16)
        v_t = vbuf[b % 2].reshape(width, chunk).astype(BF16)
        lb_new, later_new, bits_new = _sb_bits(_masked([_dot_nt(qs, kn_ref[b])], causal_new), suffix_new)
        lb_old, later_old, bits_old = _sb_bits([_dot(qs, k_t)], suffix)
        pv_new, = _sb_weighted(lb_new, later_new, [jnp.zeros((q_rows, 1), F32)], [vn_ref[b]])
        pv_old, = _sb_weighted(lb_old, later_old, bits_new, [v_t], v_transposed=True)
        carry = bits_new[0] + bits_old[0]
        acc_ref[...] = pv_new + pv_old
        carry_ref[...] = carry
        least = jnp.min(carry)

        def more(state):
            c, least = state
            return (c < n_chunks) & (least < F32_UNDERFLOW_BITS)

        def older_chunk(state):
            c, _ = state
            copies = chunk_copies(b, c, 2)
            for cp in copies:
                cp.start()
            for cp in copies:
                cp.wait()
            return c + 1, add_chunk(2)

        lax.while_loop(more, older_chunk, (1, least))
        out = jnp.zeros((t_new, width), F32)
        for h in range(heads):
            out = jnp.where(in_head[h], acc_ref[h * t_new:(h + 1) * t_new, :], out)
        o_ref[b] = out.astype(o_ref.dtype)
        return 0

    for cp in chunk_copies(0, 0, 0):
        cp.start()
    lax.fori_loop(0, n_batch, one_batch, 0)


def _attn_sample(q_bf, k_bf, v_bf, cache_k_t, cache_v_t, layer):
    n_batch, t_new, width = q_bf.shape
    heads, head_dim, past = cache_k_t.shape[2:]
    q_rows = heads * t_new
    full = lambda a: pl.BlockSpec(a.shape, lambda i: (0,) * a.ndim)
    chunk_buffers = pltpu.VMEM((3, heads, head_dim, SAMPLE_CHUNK_TOKENS), F32)
    return pl.pallas_call(
        functools.partial(_attn_sample_kernel, layer=layer, heads=heads, past=past),
        grid=(1,),
        in_specs=[full(q_bf), full(k_bf), full(v_bf),
                  pl.BlockSpec(memory_space=pl.ANY), pl.BlockSpec(memory_space=pl.ANY)],
        out_specs=full(q_bf),
        out_shape=jax.ShapeDtypeStruct(q_bf.shape, BF16),
        scratch_shapes=[chunk_buffers, chunk_buffers, pltpu.SemaphoreType.DMA((2, 3)),
                        pltpu.VMEM((q_rows, width), BF16),
                        pltpu.VMEM((q_rows, 1), F32),
                        pltpu.VMEM((q_rows, width), F32)],
        compiler_params=pltpu.CompilerParams(
            dimension_semantics=("arbitrary",), vmem_limit_bytes=V7X_VMEM_LIMIT_BYTES),
        name="attn_sample",
    )(q_bf, k_bf, v_bf, cache_k_t, cache_v_t)


def _trailing_window_sums(u_ext, axis):
    n = u_ext.shape[axis]
    t = n - HALO_ROWS
    sl = lambda a, lo, hi: lax.slice_in_dim(a, lo, hi, axis=axis)
    cols = lambda a, g: a[..., g * POOL_GROUP_WIDTH:]
    s2 = sl(u_ext, 1, n) + sl(u_ext, 0, n - 1)
    s2b = cols(s2, 1)
    s4 = sl(s2b, 2, n - 1) + sl(s2b, 0, n - 3)
    s4b = s4[..., POOL_GROUP_WIDTH:]
    s8 = sl(s4b, 4, n - 3) + sl(s4b, 0, n - 7)
    s8b = s8[..., POOL_GROUP_WIDTH:]
    s16 = sl(s8b, 8, n - 7) + sl(s8b, 0, n - 15)
    first = HALO_ROWS
    return (sl(s2, first - 1, first - 1 + t)[..., :POOL_GROUP_WIDTH],
            sl(s4, first - 3, first - 3 + t)[..., :POOL_GROUP_WIDTH],
            sl(s8, first - 7, first - 7 + t)[..., :POOL_GROUP_WIDTH],
            sl(s16, first - 15, first - 15 + t))


def _post_kernel(x_ref, u_ref, halo_ref, s_ref, mod_ref, wpool_ref, pscale_ref, wo_ref, g1_ref, b1_ref,
                 wgu_ref, wdown_ref, g2_ref, b2_ref, *refs, d_model, d_ff, alpha, ff_chunks, has_state, n_parts,
                 n_stack):
    kv_refs, (o_ref, *stacked_refs) = refs[:2 * n_stack], refs[2 * n_stack:]
    for l in range(n_stack):
        for src, dst in zip(kv_refs[2 * l:2 * l + 2], stacked_refs):
            dst[l] = src[...]
    mod = mod_ref[...]
    gate1 = mod[..., 2 * d_model:3 * d_model]
    shift2 = mod[..., 3 * d_model:4 * d_model]
    scale2 = mod[..., 4 * d_model:5 * d_model]
    gate2 = mod[..., 5 * d_model:6 * d_model]
    row_axis = x_ref.ndim - 2
    rows = x_ref.shape[row_axis] // n_parts
    assert has_state <= (n_parts == 1) and rows >= HALO_ROWS
    parts = [slice(p * rows, (p + 1) * rows) for p in range(n_parts)]
    in_part = lambda ref, part: ref[...] if has_state else ref[part, :]

    def pooled_mix_inputs(p):
        u = in_part(u_ref, parts[p])
        if has_state:
            halo = halo_ref[...]
            inv_counts = [1.0 / w for w in POOL_WINDOWS]
        else:
            if p == 0:
                halo = halo_ref[...]
                halo = jnp.where(pl.program_id(0) == 0, jnp.zeros_like(halo), halo)
            else:
                halo = u_ref[p * rows - HALO_ROWS:p * rows, :]
            t = (pl.program_id(0) * n_parts + p) * rows + lax.broadcasted_iota(jnp.int32, (rows, 1), 0)
            avail = (t + 1).astype(F32)
            inv_counts = [1.0 / jnp.minimum(avail, float(w)) for w in POOL_WINDOWS]
        sums = _trailing_window_sums(jnp.concatenate([halo, u], axis=row_axis), row_axis)
        pool_out = []
        for g, (win_sum, inv) in enumerate(zip(sums, inv_counts)):
            cur = u[..., g * POOL_GROUP_WIDTH:(g + 1) * POOL_GROUP_WIDTH]
            pooled = (win_sum * inv - cur).reshape(-1, POOL_GROUP_WIDTH)
            pool_out.append(_dot(pooled.astype(BF16), wpool_ref[g]))
        pool_out = jnp.concatenate(pool_out, axis=-1) * pscale_ref[...]
        sb_out = in_part(s_ref, parts[p])
        return pool_out.astype(BF16), sb_out.reshape(-1, sb_out.shape[-1]).astype(BF16)

    every = range(n_parts)
    xs = [in_part(x_ref, parts[p]) for p in every]
    mix_in = [pooled_mix_inputs(p) for p in every]
    mix = [_dot(pool_bf, wo_ref[0:POOL_WIDTH, :]) + _dot(sb_bf, wo_ref[POOL_WIDTH:, :]) for pool_bf, sb_bf in mix_in]
    x1 = [_layer_norm(alpha * xs[p] + gate1 * mix[p].reshape(xs[p].shape)) * g1_ref[...] + b1_ref[...] for p in every]
    h = [(_layer_norm(x1[p]) * (1.0 + scale2) + shift2).astype(BF16).reshape(-1, d_model) for p in every]
    ff = [None] * n_parts
    for lo, hi in ff_chunks:
        gate = [_dot(h[p], wgu_ref[:, lo:hi]) for p in every]
        up = [_dot(h[p], wgu_ref[:, d_ff + lo:d_ff + hi]) for p in every]
        act = [(_silu(gate[p]) * up[p]).astype(BF16) for p in every]
        part = [_dot(act[p], wdown_ref[lo:hi, :]) for p in every]
        ff = [part[p] if ff[p] is None else ff[p] + part[p] for p in every]
    for p in every:
        out = _layer_norm(alpha * x1[p] + gate2 * ff[p].reshape(xs[p].shape)) * g2_ref[...] + b2_ref[...]
        if has_state:
            o_ref[...] = out
        else:
            o_ref[parts[p], :] = out


def _ff_chunks(d_ff):
    mxu_cols = 256
    tiles = d_ff // mxu_cols
    assert tiles * mxu_cols == d_ff
    split = (tiles + 1) // 2 * mxu_cols
    return ((0, split), (split, d_ff))


def _post(x, u, halo_src, s, mod, layer, w_pool_bf, pool_scale, w_o_bf, ln1_g, ln1_b, w_gu_bf, w_down_bf,
          ln2_g, ln2_b, alpha, kv_layers=()):
    d_model = x.shape[-1]
    d_ff = w_down_bf.shape[1]
    depth = w_o_bf.shape[0]
    has_state = x.ndim == 3
    if has_state:
        grid = (1,)
        full3 = lambda a: pl.BlockSpec(a.shape, lambda i: (0, 0, 0))
        x_spec, u_spec, halo_spec, s_spec, mod_spec, out_spec = full3(x), full3(u), full3(halo_src), full3(s), full3(mod), full3(x)
    else:
        rows = x.shape[0]
        tm = ROW_TILE
        grid = (rows // tm,)
        halo_blocks = tm // HALO_ROWS
        x_spec = pl.BlockSpec((tm, d_model), lambda i: (i, 0))
        u_spec = pl.BlockSpec((tm, POOL_WIDTH), lambda i: (i, 0))
        halo_spec = pl.BlockSpec((HALO_ROWS, POOL_WIDTH), lambda i: (jnp.maximum(i * halo_blocks - 1, 0), 0))
        s_spec = pl.BlockSpec((tm, s.shape[-1]), lambda i: (i, 0))
        mod_spec = pl.BlockSpec(mod.shape, lambda i: (0, 0))
        out_spec = x_spec
    const = pl.Buffered(1)

    def layer_spec(a):
        nd = a.ndim - 1
        return pl.BlockSpec((None,) + a.shape[1:], lambda i: (layer,) + (0,) * nd, pipeline_mode=const)

    vec = lambda a: a.reshape(depth, 1, a.shape[-1])
    kv_flat = [a for pair in kv_layers for a in pair]
    out_specs, out_shape = [out_spec], [jax.ShapeDtypeStruct(x.shape, F32)]
    if kv_flat:
        sb = kv_flat[0].shape[0]
        kv_specs = [pl.BlockSpec((sb, tm), lambda i: (0, i))] * len(kv_flat)
        out_specs += [pl.BlockSpec((len(kv_layers), sb, tm), lambda i: (0, 0, i))] * 2
        out_shape += [jax.ShapeDtypeStruct((len(kv_layers),) + kv_flat[0].shape, F32)] * 2
    else:
        kv_specs = []
    return pl.pallas_call(
        functools.partial(_post_kernel, d_model=d_model, d_ff=d_ff, alpha=alpha, ff_chunks=_ff_chunks(d_ff),
                          has_state=has_state, n_parts=1 if has_state else POST_ROW_PARTS, n_stack=len(kv_layers)),
        grid=grid,
        in_specs=[x_spec, u_spec, halo_spec, s_spec, mod_spec,
                  layer_spec(w_pool_bf), layer_spec(vec(pool_scale)), layer_spec(w_o_bf),
                  layer_spec(vec(ln1_g)), layer_spec(vec(ln1_b)),
                  layer_spec(w_gu_bf), layer_spec(w_down_bf),
                  layer_spec(vec(ln2_g)), layer_spec(vec(ln2_b))] + kv_specs,
        out_specs=out_specs,
        out_shape=out_shape,
        compiler_params=pltpu.CompilerParams(
            dimension_semantics=("arbitrary",), vmem_limit_bytes=V7X_VMEM_LIMIT_BYTES),
        name="post",
    )(x, u, halo_src, s, mod, w_pool_bf, vec(pool_scale), w_o_bf, vec(ln1_g), vec(ln1_b),
      w_gu_bf, w_down_bf, vec(ln2_g), vec(ln2_b), *kv_flat)


def kernel(x_prompt, x_sample, cache_k, cache_v, state_pool, c_prompt, c_sample, w_ada, b_ada, w_in, w_pool,
           pool_scale, w_o, ln1_g, ln1_b, w_gu, w_down, ln2_g, ln2_b):
    depth, d_model, _ = w_ada.shape
    batch, seq, _ = x_prompt.shape
    dec_batch, dec_seq, _ = x_sample.shape
    past = cache_k.shape[2]
    heads, head_dim = cache_k.shape[3], cache_k.shape[4]
    assert batch == 1 and head_dim == SB_HEAD_DIM and state_pool.shape[2] == POOL_HIST
    assert seq % ROW_TILE == 0 and seq % INPROJ_ROW_TILE == 0
    assert past % SAMPLE_CHUNK_TOKENS == 0 and dec_seq >= POOL_HIST
    alpha = (2 * depth) ** 0.25

    n_cond = batch + dec_batch
    c_all = jnp.concatenate([c_prompt, c_sample], axis=0)
    c_all = jnp.pad(c_all, ((0, -n_cond % 16), (0, 0)))
    mod = _adaln(c_all, w_ada, b_ada)

    w_in_bf, w_pool_bf, w_o_bf = w_in.astype(BF16), w_pool.astype(BF16), w_o.astype(BF16)
    n_uq = w_in.shape[-1] - 2 * heads * head_dim
    w_uq_bf = w_in_bf[:, :, :n_uq]
    w_kv_t_bf = w_in_bf[:, :, n_uq:].transpose(0, 2, 1)
    w_gu_bf, w_down_bf = w_gu.astype(BF16), w_down.astype(BF16)
    cache_k_t = cache_k.transpose(0, 1, 3, 4, 2)
    cache_v_t = cache_v.transpose(0, 1, 3, 4, 2)
    hist = jnp.pad(state_pool, ((0, 0), (0, 0), (HALO_ROWS - POOL_HIST, 0), (0, 0)))

    xp = x_prompt.reshape(seq, d_model)
    xs = x_sample
    kv_p, pool_p, k_s, v_s, pool_s = [], [], [], [], []
    for l in range(depth):
        post_w = (l, w_pool_bf, pool_scale, w_o_bf, ln1_g, ln1_b, w_gu_bf, w_down_bf, ln2_g, ln2_b, alpha)

        mod_p = mod[l, 0:batch]
        u, k, v, qb, kb, vb = _inproj_prompt(xp, mod_p, w_uq_bf, w_kv_t_bf, l)
        kv_p.append((k, v))
        s = _attn_prompt(qb, kb, vb)
        if l + 1 < depth:
            xp, = _post(xp, u, u, s, mod_p, *post_w)
        else:
            xp, k_p, v_p = _post(xp, u, u, s, mod_p, *post_w, kv_layers=kv_p)
        pool_p.append(u[seq - POOL_HIST:].reshape(batch, POOL_HIST, POOL_WIDTH))

        mod_s = mod[l, batch:n_cond].reshape(dec_batch, 1, -1)
        u, k, v, qb, kb, vb = _inproj(xs, mod_s, w_in_bf, l)
        to3 = lambda a: a.reshape(dec_batch, dec_seq, a.shape[-1])
        s = _attn_sample(to3(qb), to3(kb), to3(vb), cache_k_t, cache_v_t, l)
        u3 = to3(u)
        xs, = _post(xs, u3, hist[l], s, mod_s, *post_w)
        k_s.append(k.reshape(dec_batch, dec_seq, heads, head_dim))
        v_s.append(v.reshape(dec_batch, dec_seq, heads, head_dim))
        pool_s.append(u3[:, dec_seq - POOL_HIST:])

    by_token = lambda a: a.reshape(depth, batch, heads, head_dim, seq).transpose(0, 1, 4, 2, 3)
    return (xp.reshape(batch, seq, d_model), xs, by_token(k_p), by_token(v_p), jnp.stack(pool_p),
            jnp.stack(k_s), jnp.stack(v_s), jnp.stack(pool_s))
```

```python
import functools
import math

import jax
import jax.numpy as jnp
from jax import lax
from jax.experimental import pallas as pl
from jax.experimental.pallas import tpu as pltpu

F32 = jnp.float32
BF16 = jnp.bfloat16

POOL_WINDOWS = (2, 4, 8, 16)
POOL_GROUP_WIDTH = 128
POOL_WIDTH = POOL_GROUP_WIDTH * len(POOL_WINDOWS)
POOL_HIST = max(POOL_WINDOWS) - 1
HALO_ROWS = 16
SB_HEAD_DIM = 64
SB_SCALE = 1.0 / math.sqrt(SB_HEAD_DIM)
LOG2_E = 1.0 / math.log(2.0)
HEAD_PAIR_WIDTH = 2 * SB_HEAD_DIM
LN_EPS = 1e-5
F32_UNDERFLOW_BITS = 152.0
MASKED_LOGIT = -1e4

V7X_VMEM_LIMIT_BYTES = 56 * 1024 * 1024

ROW_TILE = 512
INPROJ_ROW_TILE = 1024
POST_ROW_PARTS = 2
ATTN_TILE = 256
SAMPLE_CHUNK_TOKENS = 256
ADA_COL_TILE = 3072


def _dot(a, b):
    return jnp.dot(a, b, preferred_element_type=F32)


def _dot_nt(a, b):
    return lax.dot_general(a, b, (((1,), (1,)), ((), ())), preferred_element_type=F32)


def _split_bf16(x):
    hi = x.astype(BF16)
    lo = (x - hi.astype(F32)).astype(BF16)
    return hi, lo


def _layer_norm(x):
    mu = jnp.mean(x, axis=-1, keepdims=True)
    xc = x - mu
    var = jnp.mean(xc * xc, axis=-1, keepdims=True)
    return xc * lax.rsqrt(var + LN_EPS)


def _silu(x):
    return x / (1.0 + jnp.exp(-x))


def _softplus2(z2):
    return jnp.maximum(z2, 0.0) + LOG2_E * jnp.log(1.0 + jnp.exp2(-jnp.abs(z2)))


def _suffix_ones(n):
    r = lax.broadcasted_iota(jnp.int32, (n, n), 0)
    c = lax.broadcasted_iota(jnp.int32, (n, n), 1)
    return jnp.where(r > c, 1.0, 0.0).astype(BF16)


def _adaln_kernel(c_ref, w_ref, b_ref, o_ref):
    cond = _silu(c_ref[...])
    c_hi, c_lo = _split_bf16(cond)
    w = w_ref[...].astype(BF16)
    o_ref[...] = _dot(c_hi, w) + _dot(c_lo, w) + b_ref[...]


def _adaln(c_all, w_ada, b_ada):
    depth, d_model, n_out = w_ada.shape
    rows = c_all.shape[0]
    tn = ADA_COL_TILE
    return pl.pallas_call(
        _adaln_kernel,
        grid=(depth, n_out // tn),
        in_specs=[
            pl.BlockSpec((rows, d_model), lambda l, j: (0, 0)),
            pl.BlockSpec((None, d_model, tn), lambda l, j: (l, 0, j)),
            pl.BlockSpec((None, 1, tn), lambda l, j: (l, 0, j)),
        ],
        out_specs=pl.BlockSpec((None, rows, tn), lambda l, j: (l, 0, j)),
        out_shape=jax.ShapeDtypeStruct((depth, rows, n_out), F32),
        compiler_params=pltpu.CompilerParams(
            dimension_semantics=("arbitrary", "arbitrary"), vmem_limit_bytes=V7X_VMEM_LIMIT_BYTES),
        name="adaln",
    )(c_all, w_ada, b_ada.reshape(depth, 1, n_out))


def _inproj_kernel(x_ref, mod_ref, w_ref, u_ref, k_ref, v_ref, qb_ref, kb_ref, vb_ref, *, d_model):
    x = x_ref[...]
    mod = mod_ref[...]
    shift = mod[..., 0:d_model]
    scale = mod[..., d_model:2 * d_model]
    h = (_layer_norm(x) * (1.0 + scale) + shift).astype(BF16)
    h = h.reshape(-1, d_model)
    proj = _dot(h, w_ref[...])
    sb = (proj.shape[-1] - POOL_WIDTH) // 3
    o = POOL_WIDTH
    u = proj[:, :o]
    q = proj[:, o:o + sb]
    k = proj[:, o + sb:o + 2 * sb]
    v = proj[:, o + 2 * sb:o + 3 * sb]
    u_ref[...] = u
    k_ref[...] = k
    v_ref[...] = v
    qb_ref[...] = (q * (SB_SCALE * LOG2_E)).astype(BF16)
    kb_ref[...] = k.astype(BF16)
    vb_ref[...] = v.astype(BF16)


def _inproj(x, mod, w_in_bf, layer):
    d_model = x.shape[-1]
    n_out = w_in_bf.shape[-1]
    sb = (n_out - POOL_WIDTH) // 3
    rows = x.shape[0] * x.shape[1]
    row_spec = lambda w: pl.BlockSpec((rows, w), lambda i: (0, 0))
    return pl.pallas_call(
        functools.partial(_inproj_kernel, d_model=d_model),
        grid=(1,),
        in_specs=[pl.BlockSpec(x.shape, lambda i: (0, 0, 0)), pl.BlockSpec(mod.shape, lambda i: (0, 0, 0)),
                  pl.BlockSpec((None, d_model, n_out), lambda i: (layer, 0, 0))],
        out_specs=[row_spec(POOL_WIDTH), row_spec(sb), row_spec(sb), row_spec(sb), row_spec(sb), row_spec(sb)],
        out_shape=[jax.ShapeDtypeStruct((rows, POOL_WIDTH), F32),
                   jax.ShapeDtypeStruct((rows, sb), F32),
                   jax.ShapeDtypeStruct((rows, sb), F32),
                   jax.ShapeDtypeStruct((rows, sb), BF16),
                   jax.ShapeDtypeStruct((rows, sb), BF16),
                   jax.ShapeDtypeStruct((rows, sb), BF16)],
        compiler_params=pltpu.CompilerParams(
            dimension_semantics=("arbitrary",), vmem_limit_bytes=V7X_VMEM_LIMIT_BYTES),
        name="inproj",
    )(x, mod, w_in_bf)


def _inproj_prompt_kernel(x_ref, mod_ref, w_uq_ref, w_kv_t_ref, u_ref, k_ref, v_ref, qb_ref, kb_ref, vb_ref,
                          *, d_model, key_block):
    mod = mod_ref[...]
    shift = mod[..., 0:d_model]
    scale = mod[..., d_model:2 * d_model]
    parts = [slice(c * key_block, (c + 1) * key_block) for c in range(x_ref.shape[0] // key_block)]
    hs = [(_layer_norm(x_ref[part, :]) * (1.0 + scale) + shift).astype(BF16) for part in parts]
    uqs = [_dot(h, w_uq_ref[...]) for h in hs]
    kv_ts = [_dot_nt(w_kv_t_ref[...], h) for h in hs]
    sb = w_kv_t_ref.shape[0] // 2
    for c, (part, uq, kv_t) in enumerate(zip(parts, uqs, kv_ts)):
        u_ref[part, :] = uq[:, :POOL_WIDTH]
        qb_ref[part, :] = (uq[:, POOL_WIDTH:] * (SB_SCALE * LOG2_E)).astype(BF16)
        for f32_ref, bf_ref, val in ((k_ref, kb_ref, kv_t[:sb]), (v_ref, vb_ref, kv_t[sb:])):
            f32_ref[:, part] = val
            bf_ref[c] = val.astype(BF16)


def _inproj_prompt(x, mod, w_uq_bf, w_kv_t_bf, layer):
    seq, d_model = x.shape
    n_uq = w_uq_bf.shape[-1]
    sb = w_kv_t_bf.shape[1] // 2
    tm = INPROJ_ROW_TILE
    key_block = ATTN_TILE
    row_spec = lambda w: pl.BlockSpec((tm, w), lambda i: (i, 0))
    t_spec = pl.BlockSpec((sb, tm), lambda i: (0, i))
    blocks_spec = pl.BlockSpec((tm // key_block, sb, key_block), lambda i: (i, 0, 0))
    return pl.pallas_call(
        functools.partial(_inproj_prompt_kernel, d_model=d_model, key_block=key_block),
        grid=(seq // tm,),
        in_specs=[row_spec(d_model), pl.BlockSpec(mod.shape, lambda i: (0, 0)),
                  pl.BlockSpec((None, d_model, n_uq), lambda i: (layer, 0, 0)),
                  pl.BlockSpec((None, 2 * sb, d_model), lambda i: (layer, 0, 0))],
        out_specs=[row_spec(POOL_WIDTH), t_spec, t_spec, row_spec(sb), blocks_spec, blocks_spec],
        out_shape=[jax.ShapeDtypeStruct((seq, POOL_WIDTH), F32),
                   jax.ShapeDtypeStruct((sb, seq), F32),
                   jax.ShapeDtypeStruct((sb, seq), F32),
                   jax.ShapeDtypeStruct((seq, sb), BF16),
                   jax.ShapeDtypeStruct((seq // key_block, sb, key_block), BF16),
                   jax.ShapeDtypeStruct((seq // key_block, sb, key_block), BF16)],
        compiler_params=pltpu.CompilerParams(
            dimension_semantics=("arbitrary",), vmem_limit_bytes=V7X_VMEM_LIMIT_BYTES),
        name="inproj_prompt",
    )(x, mod, w_uq_bf, w_kv_t_bf)


def _sb_bits(z2s, suffix):
    log_betas, later, sums = [], [], []
    for z2 in z2s:
        bits = _softplus2(z2)
        log_betas.append(z2 - bits)
        later.append(_dot(bits.astype(BF16), suffix))
        sums.append(later[-1][:, :1] + bits[:, :1])
    return log_betas, later, sums


def _sb_weighted(log_betas, later, carries, v_bfs, v_transposed=False):
    pv_dot = _dot_nt if v_transposed else _dot
    return [pv_dot(jnp.exp2(lb - (l + c)).astype(BF16), v) for lb, l, c, v in zip(log_betas, later, carries, v_bfs)]


def _masked(z2s, mask):
    return z2s if mask is None else [jnp.where(mask, z2, MASKED_LOGIT) for z2 in z2s]


def _sb_blocks(z2s, v_bfs, suffix, carries, mask, v_transposed=False):
    log_betas, later, sums = _sb_bits(_masked(z2s, mask), suffix)
    return _sb_weighted(log_betas, later, carries, v_bfs, v_transposed), sums


def _sb_chain(z2s, v_bfs, suffix, carry, mask, v_transposed=False):
    log_betas, later, sums = _sb_bits(_masked(z2s, mask), suffix)
    carries = [carry]
    for block_sum in sums:
        carries.append(carries[-1] + block_sum)
    pvs = _sb_weighted(log_betas, later, carries[:-1], v_bfs, v_transposed)
    total = pvs[0]
    for pv in pvs[1:]:
        total = total + pv
    return total, carries[-1]


def _attn_prompt_kernel(q_ref, k_ref, v_ref, o_ref, carry_ref, acc_ref, *, tile, heads):
    i = pl.program_id(1)
    lane = lax.broadcasted_iota(jnp.int32, (tile, HEAD_PAIR_WIDTH), 1)
    low = lane < SB_HEAD_DIM
    q_heads = []
    for h in range(heads):
        qp = q_ref[:, (h // 2) * HEAD_PAIR_WIDTH:(h // 2 + 1) * HEAD_PAIR_WIDTH]
        q_heads.append(jnp.where(low if h % 2 == 0 else ~low, qp, jnp.zeros_like(qp)))
    suffix = _suffix_ones(tile)
    row = lax.broadcasted_iota(jnp.int32, (tile, tile), 0)
    col = lax.broadcasted_iota(jnp.int32, (tile, tile), 1)
    causal = col < row

    def block(j, mask, first):
        pair_rows = lambda h: slice((h // 2) * HEAD_PAIR_WIDTH, (h // 2 + 1) * HEAD_PAIR_WIDTH)
        z2s = [_dot(q_heads[h], k_ref[j, pair_rows(h), :]) for h in range(heads)]
        v_blks = [v_ref[j, pair_rows(h), :] for h in range(heads)]
        carries = [jnp.zeros((tile, 1), F32) if first else carry_ref[h] for h in range(heads)]
        pvs, blk_bits = _sb_blocks(z2s, v_blks, suffix, carries, mask, v_transposed=True)
        least = None
        for h in range(heads):
            carry = carries[h] + blk_bits[h]
            carry_ref[h] = carry
            acc_ref[h] = pvs[h] if first else acc_ref[h] + pvs[h]
            least = carry if least is None else jnp.minimum(least, carry)
        return jnp.min(least)

    least = block(i, causal, True)

    def more(state):
        j, least = state
        return (j >= 0) & (least < F32_UNDERFLOW_BITS)

    lax.while_loop(more, lambda state: (state[0] - 1, block(state[0], None, False)), (i - 1, least))
    for p in range(heads // 2):
        out = jnp.where(low, acc_ref[2 * p], acc_ref[2 * p + 1])
        o_ref[:, p * HEAD_PAIR_WIDTH:(p + 1) * HEAD_PAIR_WIDTH] = out.astype(o_ref.dtype)


def _attn_prompt(q_bf, k_bf, v_bf):
    seq, width = q_bf.shape
    tile = ATTN_TILE
    groups = 2
    gw = width // groups
    heads = gw // SB_HEAD_DIM
    resident = lambda: pl.BlockSpec((k_bf.shape[0], gw, tile), lambda g, i: (0, g, 0), pipeline_mode=pl.Buffered(1))
    return pl.pallas_call(
        functools.partial(_attn_prompt_kernel, tile=tile, heads=heads),
        grid=(groups, seq // tile),
        in_specs=[pl.BlockSpec((tile, gw), lambda g, i: (i, g)), resident(), resident()],
        out_specs=pl.BlockSpec((tile, gw), lambda g, i: (i, g)),
        out_shape=jax.ShapeDtypeStruct((seq, width), BF16),
        scratch_shapes=[pltpu.VMEM((heads, tile, 1), F32),
                        pltpu.VMEM((heads, tile, HEAD_PAIR_WIDTH), F32)],
        compiler_params=pltpu.CompilerParams(
            dimension_semantics=("arbitrary", "arbitrary"), vmem_limit_bytes=V7X_VMEM_LIMIT_BYTES),
        name="attn_prompt",
    )(q_bf, k_bf, v_bf)


def _attn_sample_kernel(q_ref, kn_ref, vn_ref, ck_hbm, cv_hbm, o_ref, kbuf, vbuf, sem, qs_ref, carry_ref, acc_ref,
                        *, layer, heads, past):
    n_batch, t_new, width = q_ref.shape
    q_rows = heads * t_new
    chunk = SAMPLE_CHUNK_TOKENS
    n_chunks = past // chunk
    lane = lax.broadcasted_iota(jnp.int32, (t_new, width), 1)
    in_head = [(lane >= h * SB_HEAD_DIM) & (lane < (h + 1) * SB_HEAD_DIM) for h in range(heads)]
    row = lax.broadcasted_iota(jnp.int32, (q_rows, t_new), 0)
    col = lax.broadcasted_iota(jnp.int32, (q_rows, t_new), 1)
    causal_new = col < (row % t_new)
    suffix_new = _suffix_ones(t_new)
    suffix = _suffix_ones(chunk)

    def chunk_copies(b, c, slot):
        tokens = pl.ds(pl.multiple_of(past - (c + 1) * chunk, chunk), chunk)
        return (pltpu.make_async_copy(ck_hbm.at[layer, b, :, :, tokens], kbuf.at[slot], sem.at[0, slot]),
                pltpu.make_async_copy(cv_hbm.at[layer, b, :, :, tokens], vbuf.at[slot], sem.at[1, slot]))

    def add_chunk(slot):
        k_t = kbuf[slot].reshape(width, chunk).astype(BF16)
        v_t = vbuf[slot].reshape(width, chunk).astype(BF16)
        pv, carry = _sb_chain([_dot(qs_ref[...], k_t)], [v_t], suffix, carry_ref[...], None, v_transposed=True)
        acc_ref[...] += pv
        carry_ref[...] = carry
        return jnp.min(carry)

    def one_batch(b, _):
        @pl.when(b + 1 < n_batch)
        def _():
            for cp in chunk_copies(b + 1, 0, (b + 1) % 2):
                cp.start()

        q = q_ref[b]
        for h in range(heads):
            qs_ref[h * t_new:(h + 1) * t_new, :] = jnp.where(in_head[h], q, jnp.zeros_like(q))
        for cp in chunk_copies(b, 0, b % 2):
            cp.wait()
        qs = qs_ref[...]
        k_t = kbuf[b % 2].reshape(width, chunk).astype(BF16)
        v_t = vbuf[b % 2].reshape(width, chunk).astype(BF16)
        lb_new, later_new, bits_new = _sb_bits(_masked([_dot_nt(qs, kn_ref[b])], causal_new), suffix_new)
        lb_old, later_old, bits_old = _sb_bits([_dot(qs, k_t)], suffix)
        pv_new, = _sb_weighted(lb_new, later_new, [jnp.zeros((q_rows, 1), F32)], [vn_ref[b]])
        pv_old, = _sb_weighted(lb_old, later_old, bits_new, [v_t], v_transposed=True)
        carry = bits_new[0] + bits_old[0]
        acc_ref[...] = pv_new + pv_old
        carry_ref[...] = carry
        least = jnp.min(carry)

        def more(state):
            c, least = state
            return (c < n_chunks) & (least < F32_UNDERFLOW_BITS)

        def older_chunk(state):
            c, _ = state
            copies = chunk_copies(b, c, 2)
            for cp in copies:
                cp.start()
            for cp in copies:
                cp.wait()
            return c + 1, add_chunk(2)

        lax.while_loop(more, older_chunk, (1, least))
        out = jnp.zeros((t_new, width), F32)
        for h in range(heads):
            out = jnp.where(in_head[h], acc_ref[h * t_new:(h + 1) * t_new, :], out)
        o_ref[b] = out.astype(o_ref.dtype)
        return 0

    for cp in chunk_copies(0, 0, 0):
        cp.start()
    lax.fori_loop(0, n_batch, one_batch, 0)


def _attn_sample(q_bf, k_bf, v_bf, cache_k_t, cache_v_t, layer):
    n_batch, t_new, width = q_bf.shape
    heads, head_dim, past = cache_k_t.shape[2:]
    q_rows = heads * t_new
    full = lambda a: pl.BlockSpec(a.shape, lambda i: (0,) * a.ndim)
    chunk_buffers = pltpu.VMEM((3, heads, head_dim, SAMPLE_CHUNK_TOKENS), F32)
    return pl.pallas_call(
        functools.partial(_attn_sample_kernel, layer=layer, heads=heads, past=past),
        grid=(1,),
        in_specs=[full(q_bf), full(k_bf), full(v_bf),
                  pl.BlockSpec(memory_space=pl.ANY), pl.BlockSpec(memory_space=pl.ANY)],
        out_specs=full(q_bf),
        out_shape=jax.ShapeDtypeStruct(q_bf.shape, BF16),
        scratch_shapes=[chunk_buffers, chunk_buffers, pltpu.SemaphoreType.DMA((2, 3)),
                        pltpu.VMEM((q_rows, width), BF16),
                        pltpu.VMEM((q_rows, 1), F32),
                        pltpu.VMEM((q_rows, width), F32)],
        compiler_params=pltpu.CompilerParams(
            dimension_semantics=("arbitrary",), vmem_limit_bytes=V7X_VMEM_LIMIT_BYTES),
        name="attn_sample",
    )(q_bf, k_bf, v_bf, cache_k_t, cache_v_t)


def _trailing_window_sums(u_ext, axis):
    n = u_ext.shape[axis]
    t = n - HALO_ROWS
    sl = lambda a, lo, hi: lax.slice_in_dim(a, lo, hi, axis=axis)
    cols = lambda a, g: a[..., g * POOL_GROUP_WIDTH:]
    s2 = sl(u_ext, 1, n) + sl(u_ext, 0, n - 1)
    s2b = cols(s2, 1)
    s4 = sl(s2b, 2, n - 1) + sl(s2b, 0, n - 3)
    s4b = s4[..., POOL_GROUP_WIDTH:]
    s8 = sl(s4b, 4, n - 3) + sl(s4b, 0, n - 7)
    s8b = s8[..., POOL_GROUP_WIDTH:]
    s16 = sl(s8b, 8, n - 7) + sl(s8b, 0, n - 15)
    first = HALO_ROWS
    return (sl(s2, first - 1, first - 1 + t)[..., :POOL_GROUP_WIDTH],
            sl(s4, first - 3, first - 3 + t)[..., :POOL_GROUP_WIDTH],
            sl(s8, first - 7, first - 7 + t)[..., :POOL_GROUP_WIDTH],
            sl(s16, first - 15, first - 15 + t))


def _post_kernel(x_ref, u_ref, halo_ref, s_ref, mod_ref, wpool_ref, pscale_ref, wo_ref, g1_ref, b1_ref,
                 wgu_ref, wdown_ref, g2_ref, b2_ref, *refs, d_model, d_ff, alpha, ff_chunks, has_state, n_parts,
                 n_stack):
    kv_refs, (o_ref, *stacked_refs) = refs[:2 * n_stack], refs[2 * n_stack:]
    for l in range(n_stack):
        for src, dst in zip(kv_refs[2 * l:2 * l + 2], stacked_refs):
            dst[l] = src[...]
    mod = mod_ref[...]
    gate1 = mod[..., 2 * d_model:3 * d_model]
    shift2 = mod[..., 3 * d_model:4 * d_model]
    scale2 = mod[..., 4 * d_model:5 * d_model]
    gate2 = mod[..., 5 * d_model:6 * d_model]
    row_axis = x_ref.ndim - 2
    rows = x_ref.shape[row_axis] // n_parts
    assert has_state <= (n_parts == 1) and rows >= HALO_ROWS
    parts = [slice(p * rows, (p + 1) * rows) for p in range(n_parts)]
    in_part = lambda ref, part: ref[...] if has_state else ref[part, :]

    def pooled_mix_inputs(p):
        u = in_part(u_ref, parts[p])
        if has_state:
            halo = halo_ref[...]
            inv_counts = [1.0 / w for w in POOL_WINDOWS]
        else:
            if p == 0:
                halo = halo_ref[...]
                halo = jnp.where(pl.program_id(0) == 0, jnp.zeros_like(halo), halo)
            else:
                halo = u_ref[p * rows - HALO_ROWS:p * rows, :]
            t = (pl.program_id(0) * n_parts + p) * rows + lax.broadcasted_iota(jnp.int32, (rows, 1), 0)
            avail = (t + 1).astype(F32)
            inv_counts = [1.0 / jnp.minimum(avail, float(w)) for w in POOL_WINDOWS]
        sums = _trailing_window_sums(jnp.concatenate([halo, u], axis=row_axis), row_axis)
        pool_out = []
        for g, (win_sum, inv) in enumerate(zip(sums, inv_counts)):
            cur = u[..., g * POOL_GROUP_WIDTH:(g + 1) * POOL_GROUP_WIDTH]
            pooled = (win_sum * inv - cur).reshape(-1, POOL_GROUP_WIDTH)
            pool_out.append(_dot(pooled.astype(BF16), wpool_ref[g]))
        pool_out = jnp.concatenate(pool_out, axis=-1) * pscale_ref[...]
        sb_out = in_part(s_ref, parts[p])
        return pool_out.astype(BF16), sb_out.reshape(-1, sb_out.shape[-1]).astype(BF16)

    every = range(n_parts)
    xs = [in_part(x_ref, parts[p]) for p in every]
    mix_in = [pooled_mix_inputs(p) for p in every]
    mix = [_dot(pool_bf, wo_ref[0:POOL_WIDTH, :]) + _dot(sb_bf, wo_ref[POOL_WIDTH:, :]) for pool_bf, sb_bf in mix_in]
    x1 = [_layer_norm(alpha * xs[p] + gate1 * mix[p].reshape(xs[p].shape)) * g1_ref[...] + b1_ref[...] for p in every]
    h = [(_layer_norm(x1[p]) * (1.0 + scale2) + shift2).astype(BF16).reshape(-1, d_model) for p in every]
    ff = [None] * n_parts
    for lo, hi in ff_chunks:
        gate = [_dot(h[p], wgu_ref[:, lo:hi]) for p in every]
        up = [_dot(h[p], wgu_ref[:, d_ff + lo:d_ff + hi]) for p in every]
        act = [(_silu(gate[p]) * up[p]).astype(BF16) for p in every]
        part = [_dot(act[p], wdown_ref[lo:hi, :]) for p in every]
        ff = [part[p] if ff[p] is None else ff[p] + part[p] for p in every]
    for p in every:
        out = _layer_norm(alpha * x1[p] + gate2 * ff[p].reshape(xs[p].shape)) * g2_ref[...] + b2_ref[...]
        if has_state:
            o_ref[...] = out
        else:
            o_ref[parts[p], :] = out


def _ff_chunks(d_ff):
    mxu_cols = 256
    tiles = d_ff // mxu_cols
    assert tiles * mxu_cols == d_ff
    split = (tiles + 1) // 2 * mxu_cols
    return ((0, split), (split, d_ff))


def _post(x, u, halo_src, s, mod, layer, w_pool_bf, pool_scale, w_o_bf, ln1_g, ln1_b, w_gu_bf, w_down_bf,
          ln2_g, ln2_b, alpha, kv_layers=()):
    d_model = x.shape[-1]
    d_ff = w_down_bf.shape[1]
    depth = w_o_bf.shape[0]
    has_state = x.ndim == 3
    if has_state:
        grid = (1,)
        full3 = lambda a: pl.BlockSpec(a.shape, lambda i: (0, 0, 0))
        x_spec, u_spec, halo_spec, s_spec, mod_spec, out_spec = full3(x), full3(u), full3(halo_src), full3(s), full3(mod), full3(x)
    else:
        rows = x.shape[0]
        tm = ROW_TILE
        grid = (rows // tm,)
        halo_blocks = tm // HALO_ROWS
        x_spec = pl.BlockSpec((tm, d_model), lambda i: (i, 0))
        u_spec = pl.BlockSpec((tm, POOL_WIDTH), lambda i: (i, 0))
        halo_spec = pl.BlockSpec((HALO_ROWS, POOL_WIDTH), lambda i: (jnp.maximum(i * halo_blocks - 1, 0), 0))
        s_spec = pl.BlockSpec((tm, s.shape[-1]), lambda i: (i, 0))
        mod_spec = pl.BlockSpec(mod.shape, lambda i: (0, 0))
        out_spec = x_spec
    const = pl.Buffered(1)

    def layer_spec(a):
        nd = a.ndim - 1
        return pl.BlockSpec((None,) + a.shape[1:], lambda i: (layer,) + (0,) * nd, pipeline_mode=const)

    vec = lambda a: a.reshape(depth, 1, a.shape[-1])
    kv_flat = [a for pair in kv_layers for a in pair]
    out_specs, out_shape = [out_spec], [jax.ShapeDtypeStruct(x.shape, F32)]
    if kv_flat:
        sb = kv_flat[0].shape[0]
        kv_specs = [pl.BlockSpec((sb, tm), lambda i: (0, i))] * len(kv_flat)
        out_specs += [pl.BlockSpec((len(kv_layers), sb, tm), lambda i: (0, 0, i))] * 2
        out_shape += [jax.ShapeDtypeStruct((len(kv_layers),) + kv_flat[0].shape, F32)] * 2
    else:
        kv_specs = []
    return pl.pallas_call(
        functools.partial(_post_kernel, d_model=d_model, d_ff=d_ff, alpha=alpha, ff_chunks=_ff_chunks(d_ff),
                          has_state=has_state, n_parts=1 if has_state else POST_ROW_PARTS, n_stack=len(kv_layers)),
        grid=grid,
        in_specs=[x_spec, u_spec, halo_spec, s_spec, mod_spec,
                  layer_spec(w_pool_bf), layer_spec(vec(pool_scale)), layer_spec(w_o_bf),
                  layer_spec(vec(ln1_g)), layer_spec(vec(ln1_b)),
                  layer_spec(w_gu_bf), layer_spec(w_down_bf),
                  layer_spec(vec(ln2_g)), layer_spec(vec(ln2_b))] + kv_specs,
        out_specs=out_specs,
        out_shape=out_shape,
        compiler_params=pltpu.CompilerParams(
            dimension_semantics=("arbitrary",), vmem_limit_bytes=V7X_VMEM_LIMIT_BYTES),
        name="post",
    )(x, u, halo_src, s, mod, w_pool_bf, vec(pool_scale), w_o_bf, vec(ln1_g), vec(ln1_b),
      w_gu_bf, w_down_bf, vec(ln2_g), vec(ln2_b), *kv_flat)


def kernel(x_prompt, x_sample, cache_k, cache_v, state_pool, c_prompt, c_sample, w_ada, b_ada, w_in, w_pool,
           pool_scale, w_o, ln1_g, ln1_b, w_gu, w_down, ln2_g, ln2_b):
    depth, d_model, _ = w_ada.shape
    batch, seq, _ = x_prompt.shape
    dec_batch, dec_seq, _ = x_sample.shape
    past = cache_k.shape[2]
    heads, head_dim = cache_k.shape[3], cache_k.shape[4]
    assert batch == 1 and head_dim == SB_HEAD_DIM and state_pool.shape[2] == POOL_HIST
    assert seq % ROW_TILE == 0 and seq % INPROJ_ROW_TILE == 0
    assert past % SAMPLE_CHUNK_TOKENS == 0 and dec_seq >= POOL_HIST
    alpha = (2 * depth) ** 0.25

    n_cond = batch + dec_batch
    c_all = jnp.concatenate([c_prompt, c_sample], axis=0)
    c_all = jnp.pad(c_all, ((0, -n_cond % 16), (0, 0)))
    mod = _adaln(c_all, w_ada, b_ada)

    w_in_bf, w_pool_bf, w_o_bf = w_in.astype(BF16), w_pool.astype(BF16), w_o.astype(BF16)
    n_uq = w_in.shape[-1] - 2 * heads * head_dim
    w_uq_bf = w_in_bf[:, :, :n_uq]
    w_kv_t_bf = w_in_bf[:, :, n_uq:].transpose(0, 2, 1)
    w_gu_bf, w_down_bf = w_gu.astype(BF16), w_down.astype(BF16)
    cache_k_t = cache_k.transpose(0, 1, 3, 4, 2)
    cache_v_t = cache_v.transpose(0, 1, 3, 4, 2)
    hist = jnp.pad(state_pool, ((0, 0), (0, 0), (HALO_ROWS - POOL_HIST, 0), (0, 0)))

    xp = x_prompt.reshape(seq, d_model)
    xs = x_sample
    kv_p, pool_p, k_s, v_s, pool_s = [], [], [], [], []
    for l in range(depth):
        post_w = (l, w_pool_bf, pool_scale, w_o_bf, ln1_g, ln1_b, w_gu_bf, w_down_bf, ln2_g, ln2_b, alpha)

        mod_p = mod[l, 0:batch]
        u, k, v, qb, kb, vb = _inproj_prompt(xp, mod_p, w_uq_bf, w_kv_t_bf, l)
        kv_p.append((k, v))
        s = _attn_prompt(qb, kb, vb)
        if l + 1 < depth:
            xp, = _post(xp, u, u, s, mod_p, *post_w)
        else:
            xp, k_p, v_p = _post(xp, u, u, s, mod_p, *post_w, kv_layers=kv_p)
        pool_p.append(u[seq - POOL_HIST:].reshape(batch, POOL_HIST, POOL_WIDTH))

        mod_s = mod[l, batch:n_cond].reshape(dec_batch, 1, -1)
        u, k, v, qb, kb, vb = _inproj(xs, mod_s, w_in_bf, l)
        to3 = lambda a: a.reshape(dec_batch, dec_seq, a.shape[-1])
        s = _attn_sample(to3(qb), to3(kb), to3(vb), cache_k_t, cache_v_t, l)
        u3 = to3(u)
        xs, = _post(xs, u3, hist[l], s, mod_s, *post_w)
        k_s.append(k.reshape(dec_batch, dec_seq, heads, head_dim))
        v_s.append(v.reshape(dec_batch, dec_seq, heads, head_dim))
        pool_s.append(u3[:, dec_seq - POOL_HIST:])

    by_token = lambda a: a.reshape(depth, batch, heads, head_dim, seq).transpose(0, 1, 4, 2, 3)
    return (xp.reshape(batch, seq, d_model), xs, by_token(k_p), by_token(v_p), jnp.stack(pool_p),
            jnp.stack(k_s), jnp.stack(v_s), jnp.stack(pool_s))
```
